```python
import jax, jax.numpy as jnp
from jax import lax
import numpy as np

D_MODEL = 1024
BATCH = 2
SEQ = 8192
DEPTH = 4
DEC_BATCH = 128
DEC_SEQ = 1
PAST_LEN = 8192
PAGE_SIZE = 128

N_META = 16
N_A = DEPTH // 2
N_B = DEPTH - N_A
POOL_WINDOWS = (2, 4, 8, 16)
N_POOL_GROUPS = len(POOL_WINDOWS)
POOL_GC = D_MODEL // N_POOL_GROUPS
MAX_POOL = max(POOL_WINDOWS)
POOL_STATE = MAX_POOL - 1
HEAD_DIM = 64
N_HEADS = D_MODEL // HEAD_DIM
N_KV_HEADS = 4
GROUP = N_HEADS // N_KV_HEADS
KV_DIM = N_KV_HEADS * HEAD_DIM
WINDOW = 128
BLOCK = 128
ATTN_SCALE = HEAD_DIM ** -0.5
D_FF = 2816
CONV_W = 3
ALPHA = (2.0 * DEPTH) ** 0.25
BETA = (8.0 * DEPTH) ** -0.25
LN_EPS = 1e-5
NEG = -1e30

kernel_name = "yoco_pool_swa_sink_convffn_step"


def layer_norm(x, g, b):
    xf = x.astype(jnp.float32)
    mu = jnp.mean(xf, axis=-1, keepdims=True)
    var = jnp.mean(jnp.square(xf - mu), axis=-1, keepdims=True)
    y = (xf - mu) * lax.rsqrt(var + LN_EPS) * g.astype(jnp.float32) + b.astype(jnp.float32)
    return y.astype(x.dtype)


def pool_mixer(xs, w_pool, scale):
    L = xs.shape[1]
    xf = xs.astype(jnp.float32)
    cs = jnp.pad(jnp.cumsum(xf, axis=1), ((0, 0), (MAX_POOL, 0), (0, 0)))
    t = jnp.arange(L)
    parts = []
    for gi, w in enumerate(POOL_WINDOWS):
        ch = slice(gi * POOL_GC, (gi + 1) * POOL_GC)
        win = cs[:, MAX_POOL:MAX_POOL + L, ch] - cs[:, MAX_POOL - w:MAX_POOL - w + L, ch]
        cnt = jnp.minimum(w, t + 1).astype(jnp.float32)[None, :, None]
        parts.append(win / cnt - xf[..., ch])
    d = jnp.stack(parts, axis=2).astype(xs.dtype)
    y = jnp.einsum('blgc,gce->blge', d, w_pool).reshape(xs.shape)
    return y * scale


def conv_ffn(xs, prefix, w_in, conv_w, conv_b, w_out):
    S = xs.shape[1]
    gu = xs @ w_in
    g, u = gu[..., :D_FF], gu[..., D_FF:]
    gp = jnp.concatenate([prefix.astype(g.dtype), g], axis=1)
    c = conv_b
    for k in range(CONV_W):
        c = c + gp[:, k:k + S] * conv_w[k]
    h = jax.nn.silu(c) * u
    return h @ w_out, gp[:, -(CONV_W - 1):]


def sink_attention(q, k, v, mask, sinks):
    s = jnp.einsum('...qkgd,...skd->...kgqs', q, k).astype(jnp.float32) * ATTN_SCALE
    s = jnp.where(mask, s, NEG)
    sink = sinks.astype(jnp.float32).reshape(N_KV_HEADS, GROUP, 1, 1)
    m = jnp.maximum(jnp.max(s, axis=-1, keepdims=True), sink)
    e = jnp.exp(s - m)
    den = jnp.sum(e, axis=-1, keepdims=True) + jnp.exp(sink - m)
    p = (e / den).astype(v.dtype)
    return jnp.einsum('...kgqs,...skd->...qkgd', p, v)


def band_blocks(t, pad):
    B = t.shape[0]
    t = jnp.pad(t, ((0, 0), (pad, 0), (0, 0), (0, 0)))
    nb = t.shape[1] // BLOCK
    t = t.reshape(B, nb, BLOCK, N_KV_HEADS, HEAD_DIM)
    prev = jnp.pad(t[:, :-1], ((0, 0), (1, 0), (0, 0), (0, 0), (0, 0)))
    return jnp.concatenate([prev, t], axis=2)


def band_mask(nb, pad):
    n = jnp.arange(nb)[:, None, None]
    i = jnp.arange(BLOCK)[None, :, None]
    j = jnp.arange(2 * BLOCK)[None, None, :]
    qp = n * BLOCK + i
    kp = (n - 1) * BLOCK + j
    return (kp <= qp) & (qp - kp < WINDOW) & (kp >= pad)


def swa_prompt(x, k_band, v_band, mask, pad, w_q, b_q, sinks, w_o, b_o):
    B, L, _ = x.shape
    q = (x @ w_q + b_q).reshape(B, L, N_KV_HEADS, GROUP, HEAD_DIM)
    q = jnp.pad(q, ((0, 0), (pad, 0), (0, 0), (0, 0), (0, 0)))
    Lp = L + pad
    q = q.reshape(B, Lp // BLOCK, BLOCK, N_KV_HEADS, GROUP, HEAD_DIM)
    o = sink_attention(q, k_band, v_band, mask[None, :, None, None], sinks)
    o = o.reshape(B, Lp, N_HEADS * HEAD_DIM)[:, pad:]
    return o @ w_o + b_o


def swa_sample(x, k_all, v_all, mask, w_q, b_q, sinks, w_o, b_o):
    DB, S, _ = x.shape
    q = (x @ w_q + b_q).reshape(DB, S, N_KV_HEADS, GROUP, HEAD_DIM)
    o = sink_attention(q, k_all, v_all, mask[None, None, None], sinks)
    return o.reshape(DB, S, N_HEADS * HEAD_DIM) @ w_o + b_o


def setup_inputs(seed: int = 0) -> dict:
    key = jax.random.key(seed)
    ks = jax.random.split(key, 26)
    nrm = jax.random.normal
    f32 = jnp.float32
    kv_col_scale = jnp.concatenate([jnp.ones((KV_DIM,), f32), jnp.full((KV_DIM,), BETA, f32)])
    return {
        "x_prompt": nrm(ks[0], (BATCH, SEQ, D_MODEL), f32),
        "x_sample": nrm(ks[1], (DEC_BATCH, DEC_SEQ, D_MODEL), f32),
        "state_pool": nrm(ks[2], (N_A, DEC_BATCH, POOL_STATE, D_MODEL), f32),
        "state_conv": nrm(ks[3], (DEPTH, DEC_BATCH, CONV_W - 1, D_FF), f32),
        "state_k_win": nrm(ks[4], (DEC_BATCH, WINDOW, N_KV_HEADS, HEAD_DIM), f32),
        "state_v_win": nrm(ks[5], (DEC_BATCH, WINDOW, N_KV_HEADS, HEAD_DIM), f32),
        "meta_tokens": nrm(ks[6], (N_META, D_MODEL), f32),
        "pool_w": nrm(ks[7], (N_A, N_POOL_GROUPS, POOL_GC, POOL_GC), f32) * (POOL_GC ** -0.5) * BETA,
        "pool_scale": 1.0 + 0.1 * nrm(ks[8], (N_A, D_MODEL), f32),
        "w_kv": nrm(ks[9], (D_MODEL, 2 * KV_DIM), f32) * (D_MODEL ** -0.5) * kv_col_scale,
        "b_kv": 0.02 * nrm(ks[10], (2 * KV_DIM,), f32),
        "attn_w_q": nrm(ks[11], (N_B, D_MODEL, N_HEADS * HEAD_DIM), f32) * (D_MODEL ** -0.5),
        "attn_b_q": 0.02 * nrm(ks[12], (N_B, N_HEADS * HEAD_DIM), f32),
        "attn_sinks": 0.5 * nrm(ks[13], (N_B, N_HEADS), f32),
        "attn_w_o": nrm(ks[14], (N_B, N_HEADS * HEAD_DIM, D_MODEL), f32) * ((N_HEADS * HEAD_DIM) ** -0.5) * BETA,
        "attn_b_o": 0.02 * nrm(ks[15], (N_B, D_MODEL), f32),
        "ffn_w_in": nrm(ks[16], (DEPTH, D_MODEL, 2 * D_FF), f32) * (D_MODEL ** -0.5),
        "ffn_conv_w": nrm(ks[17], (DEPTH, CONV_W, D_FF), f32) * (CONV_W ** -0.5),
        "ffn_conv_b": 0.02 * nrm(ks[18], (DEPTH, D_FF), f32),
        "ffn_w_out": nrm(ks[19], (DEPTH, D_FF, D_MODEL), f32) * (D_FF ** -0.5) * BETA,
        "ln_mix_g": 1.0 + 0.05 * nrm(ks[20], (DEPTH, D_MODEL), f32),
        "ln_mix_b": 0.02 * nrm(ks[21], (DEPTH, D_MODEL), f32),
        "ln_ffn_g": 1.0 + 0.05 * nrm(ks[22], (DEPTH, D_MODEL), f32),
        "ln_ffn_b": 0.02 * nrm(ks[23], (DEPTH, D_MODEL), f32),
    }


def reference(x_prompt, x_sample, state_pool, state_conv, state_k_win, state_v_win,
              meta_tokens, pool_w, pool_scale, w_kv, b_kv,
              attn_w_q, attn_b_q, attn_sinks, attn_w_o, attn_b_o,
              ffn_w_in, ffn_conv_w, ffn_conv_b, ffn_w_out,
              ln_mix_g, ln_mix_b, ln_ffn_g, ln_ffn_b):
    B = x_prompt.shape[0]
    S = x_sample.shape[1]
    meta = jnp.broadcast_to(meta_tokens[None].astype(x_prompt.dtype), (B, N_META, D_MODEL))
    hp = jnp.concatenate([meta, x_prompt], axis=1)
    hs = x_sample
    L = hp.shape[1]
    pad = (-N_META) % BLOCK
    nb = (L + pad) // BLOCK

    new_pool_p, new_pool_s, new_conv_p, new_conv_s = [], [], [], []
    for layer in range(DEPTH):
        if layer < N_A:
            a = layer
            mix_p = pool_mixer(hp, pool_w[a], pool_scale[a])
            ps = jnp.concatenate([state_pool[a].astype(hs.dtype), hs], axis=1)
            mix_s = pool_mixer(ps, pool_w[a], pool_scale[a])[:, -S:]
            new_pool_p.append(hp[:, -POOL_STATE:])
            new_pool_s.append(ps[:, -POOL_STATE:])
        else:
            bi = layer - N_A
            mix_p = swa_prompt(hp, k_band, v_band, mask_p, pad, attn_w_q[bi], attn_b_q[bi],
                               attn_sinks[bi], attn_w_o[bi], attn_b_o[bi])
            mix_s = swa_sample(hs, k_all, v_all, mask_s, attn_w_q[bi], attn_b_q[bi],
                               attn_sinks[bi], attn_w_o[bi], attn_b_o[bi])
        hp = layer_norm(ALPHA * hp + mix_p, ln_mix_g[layer], ln_mix_b[layer])
        hs = layer_norm(ALPHA * hs + mix_s, ln_mix_g[layer], ln_mix_b[layer])

        zero_prefix = jnp.zeros((B, CONV_W - 1, D_FF), hp.dtype)
        f_p, c_p = conv_ffn(hp, zero_prefix, ffn_w_in[layer], ffn_conv_w[layer], ffn_conv_b[layer], ffn_w_out[layer])
        f_s, c_s = conv_ffn(hs, state_conv[layer], ffn_w_in[layer], ffn_conv_w[layer], ffn_conv_b[layer], ffn_w_out[layer])
        new_conv_p.append(c_p)
        new_conv_s.append(c_s)
        hp = layer_norm(ALPHA * hp + f_p, ln_ffn_g[layer], ln_ffn_b[layer])
        hs = layer_norm(ALPHA * hs + f_s, ln_ffn_g[layer], ln_ffn_b[layer])

        if layer == N_A - 1:
            kv_p = (hp @ w_kv + b_kv).reshape(B, L, 2, N_KV_HEADS, HEAD_DIM)
            k_p, v_p = kv_p[:, :, 0], kv_p[:, :, 1]
            new_k_p = k_p[:, -WINDOW:]
            new_v_p = v_p[:, -WINDOW:]
            k_band = band_blocks(k_p, pad)
            v_band = band_blocks(v_p, pad)
            mask_p = band_mask(nb, pad)
            kv_s = (hs @ w_kv + b_kv).reshape(hs.shape[0], S, 2, N_KV_HEADS, HEAD_DIM)
            k_all = jnp.concatenate([state_k_win.astype(hs.dtype), kv_s[:, :, 0]], axis=1)
            v_all = jnp.concatenate([state_v_win.astype(hs.dtype), kv_s[:, :, 1]], axis=1)
            new_k_s = k_all[:, -WINDOW:]
            new_v_s = v_all[:, -WINDOW:]
            qpos = jnp.arange(S)[:, None]
            kpos = (jnp.arange(WINDOW + S) - WINDOW)[None, :]
            mask_s = (kpos <= qpos) & (qpos - kpos < WINDOW)

    y_prompt = hp[:, N_META:]
    y_sample = hs
    return (y_prompt, y_sample,
            jnp.stack(new_pool_p), jnp.stack(new_pool_s),
            jnp.stack(new_conv_p), jnp.stack(new_conv_s),
            new_k_p, new_v_p, new_k_s, new_v_s)
```

```python
import functools

import jax
import jax.numpy as jnp
from jax import lax
from jax.experimental import pallas as pl
from jax.experimental.pallas import tpu as pltpu

F32 = jnp.float32
BF16 = jnp.bfloat16

D_MODEL = 1024
DEPTH = 4
N_META = 16
N_A = DEPTH // 2
POOL_WINDOWS = (2, 4, 8, 16)
N_POOL_GROUPS = len(POOL_WINDOWS)
POOL_GC = D_MODEL // N_POOL_GROUPS
MAX_POOL = max(POOL_WINDOWS)
POOL_STATE = MAX_POOL - 1
HEAD_DIM = 64
N_HEADS = D_MODEL // HEAD_DIM
N_KV_HEADS = 4
GROUP = N_HEADS // N_KV_HEADS
KV_DIM = N_KV_HEADS * HEAD_DIM
WINDOW = 128
BLOCK = 128
ATTN_SCALE = HEAD_DIM ** -0.5
D_FF = 2816
CONV_W = 3
ALPHA = (2.0 * DEPTH) ** 0.25
LN_EPS = 1e-5
NEG = -1e30

SUBLANES = 8
LANES = 128
PAD_ROWS = BLOCK - N_META
FF_CHUNK = 256
N_FF_CHUNKS = D_FF // FF_CHUNK
ROW_TILE = 512
SAMPLE_GROUP = 16
DUP = 2 * HEAD_DIM
VMEM_LIMIT = 58 * 1024 * 1024


def _layer_norm(z, g, b):
    mu = jnp.mean(z, axis=-1, keepdims=True)
    zc = z - mu
    var = jnp.mean(zc * zc, axis=-1, keepdims=True)
    return zc * lax.rsqrt(var + LN_EPS) * g + b


def _silu(c):
    return c * (1.0 / (1.0 + jnp.exp(-c)))


def _dot(a, b):
    return jnp.dot(a, b, preferred_element_type=F32)


def _ffn_chunks(h1b, wi_ref, cw_ref, cb_ref, wo_ref, taps):
    acc = None
    for c in range(N_FF_CHUNKS):
        cols = slice(c * FF_CHUNK, (c + 1) * FF_CHUNK)
        g = _dot(h1b, wi_ref[:, cols])
        u = _dot(h1b, wi_ref[:, D_FF + c * FF_CHUNK:D_FF + (c + 1) * FF_CHUNK])
        g, g1, g2 = taps(c, cols, g)
        cc = (cb_ref[:, cols] + g2 * cw_ref[0:1, cols] + g1 * cw_ref[1:2, cols]
              + g * cw_ref[2:3, cols])
        hh = (_silu(cc) * u).astype(BF16)
        part = _dot(hh, wo_ref[cols, :])
        acc = part if acc is None else acc + part
    return acc


def _prompt_kernel(*refs, kind, is_meta, emit_kv, tile, n_tiles):
    it = iter(refs)
    x_ref = next(it)
    halo_in_ref = next(it) if kind == "pool" else None
    gcar_in_ref = next(it)
    if kind == "pool":
        pw_ref, ps_ref = next(it), next(it)
    else:
        kcur_ref, vcur_ref, kprev_ref, vprev_ref, kmeta_ref, vmeta_ref = (next(it) for _ in range(6))
        mask_ref, sinks_ref = next(it), next(it)
        wq_ref, bq_ref, wao_ref, bao_ref = (next(it) for _ in range(4))
    lmg_ref, lmb_ref, lfg_ref, lfb_ref = (next(it) for _ in range(4))
    wi_ref, cw_ref, cb_ref, wo_ref = (next(it) for _ in range(4))
    if emit_kv:
        wkvd_ref, bkvd_ref, wkv_ref, bkv_ref = (next(it) for _ in range(4))
    y_ref, gtail_ref = next(it), next(it)
    if emit_kv:
        kd_ref, vd_ref, kvtail_ref = next(it), next(it), next(it)
    gcar_ref = next(it)
    if kind == "pool":
        halo_ref = next(it)
    else:
        kbuf_ref, vbuf_ref, qb_ref, ob_ref = (next(it) for _ in range(4))

    i = pl.program_id(1)
    T = tile
    x = x_ref[0]
    rows = lax.broadcasted_iota(jnp.int32, (T, 1), 0)
    live = rows >= PAD_ROWS

    @pl.when(i == 0)
    def _():
        gcar_ref[...] = gcar_in_ref[...]
        if kind == "pool":
            halo_ref[...] = halo_in_ref[...]

    if kind == "pool":
        if is_meta:
            x = jnp.where(live, x, 0.0)
        xe = jnp.concatenate([halo_ref[...], x], axis=0)
        s2 = xe + pltpu.roll(xe, 1, 0)
        s2r = s2[:, POOL_GC:]
        s4 = s2r + pltpu.roll(s2r, 2, 0)
        s4r = s4[:, POOL_GC:]
        s8 = s4r + pltpu.roll(s4r, 4, 0)
        s8r = s8[:, POOL_GC:]
        s16 = s8r + pltpu.roll(s8r, 8, 0)
        wins = (s2[MAX_POOL:, :POOL_GC], s4[MAX_POOL:, :POOL_GC],
                s8[MAX_POOL:, :POOL_GC], s16[MAX_POOL:])
        halo_ref[...] = x[T - MAX_POOL:, :]
        parts = []
        for gi, w in enumerate(POOL_WINDOWS):
            xg = x[:, gi * POOL_GC:(gi + 1) * POOL_GC]
            if is_meta:
                cnt = jnp.clip(rows - (PAD_ROWS - 1), 1, w).astype(F32)
                dg = wins[gi] / cnt - xg
            else:
                dg = wins[gi] * (1.0 / w) - xg
            parts.append(_dot(dg.astype(BF16), pw_ref[gi]))
        mix = jnp.concatenate(parts, axis=1) * ps_ref[...]
    else:
        nblk = T // BLOCK
        qb_ref[...] = (_dot(x.astype(BF16), wq_ref[...]) + bq_ref[...]).astype(BF16)

        @pl.when(i == 0)
        def _():
            kbuf_ref[0:BLOCK, :] = kmeta_ref[...]
            vbuf_ref[0:BLOCK, :] = vmeta_ref[...]

        @pl.when(i > 0)
        def _():
            kbuf_ref[0:BLOCK, :] = kprev_ref[0]
            vbuf_ref[0:BLOCK, :] = vprev_ref[0]

        kbuf_ref[BLOCK:, :] = kcur_ref[0]
        vbuf_ref[BLOCK:, :] = vcur_ref[0]
        lo = lax.broadcasted_iota(jnp.int32, (BLOCK, LANES), 1) < HEAD_DIM
        for j in range(nblk):
            r0 = j * BLOCK
            if is_meta:
                madd = mask_ref[2]
            elif j == 0:
                madd = mask_ref[jnp.where(i == 0, 1, 0)]
            else:
                madd = mask_ref[0]
            for h in range(N_KV_HEADS):
                c0 = h * GROUP * HEAD_DIM
                qa = qb_ref[r0:r0 + BLOCK, c0:c0 + LANES]
                qc = qb_ref[r0:r0 + BLOCK, c0 + LANES:c0 + 2 * LANES]
                z = jnp.zeros_like(qa)
                lhs = jnp.concatenate([jnp.where(lo, qa, z), jnp.where(lo, z, qa),
                                       jnp.where(lo, qc, z), jnp.where(lo, z, qc)], axis=0)
                kk = kbuf_ref[r0:r0 + 2 * BLOCK, h * DUP:(h + 1) * DUP]
                vv = vbuf_ref[r0:r0 + 2 * BLOCK, h * DUP:(h + 1) * DUP]
                s = lax.dot_general(lhs, kk, (((1,), (1,)), ((), ())),
                                    preferred_element_type=F32) * ATTN_SCALE
                ps = []
                for g in range(GROUP):
                    sg = s[g * BLOCK:(g + 1) * BLOCK] + madd
                    sink = sinks_ref[h * GROUP + g]
                    m = jnp.maximum(jnp.max(sg, axis=-1, keepdims=True), sink)
                    e = jnp.exp(sg - m)
                    den = jnp.sum(e, axis=-1, keepdims=True) + jnp.exp(sink - m)
                    ps.append((e / den).astype(BF16))
                res = _dot(jnp.concatenate(ps, axis=0), vv)
                oa = jnp.where(lo, res[0:BLOCK], res[BLOCK:2 * BLOCK])
                oc = jnp.where(lo, res[2 * BLOCK:3 * BLOCK], res[3 * BLOCK:])
                ob_ref[r0:r0 + BLOCK, c0:c0 + LANES] = oa.astype(BF16)
                ob_ref[r0:r0 + BLOCK, c0 + LANES:c0 + 2 * LANES] = oc.astype(BF16)
        mix = _dot(ob_ref[...], wao_ref[...]) + bao_ref[...]

    h1 = _layer_norm(ALPHA * x + mix, lmg_ref[...], lmb_ref[...])
    h1b = h1.astype(BF16)

    def taps(c, cols, g):
        if is_meta:
            g = jnp.where(live, g, 0.0)
        gext = jnp.concatenate([gcar_ref[:, cols], g], axis=0)
        g1 = pltpu.roll(gext, 1, 0)[SUBLANES:]
        g2 = pltpu.roll(gext, 2, 0)[SUBLANES:]
        gcar_ref[:, cols] = g[T - SUBLANES:]
        return g, g1, g2

    f = _ffn_chunks(h1b, wi_ref, cw_ref, cb_ref, wo_ref, taps)
    h2 = _layer_norm(ALPHA * h1 + f, lfg_ref[...], lfb_ref[...])
    y_ref[0] = h2
    gtail_ref[0] = gcar_ref[...]

    if emit_kv:
        h2b = h2.astype(BF16)
        kvd = _dot(h2b, wkvd_ref[...]) + bkvd_ref[...]
        kd_ref[0] = kvd[:, :N_KV_HEADS * DUP].astype(BF16)
        vd_ref[0] = kvd[:, N_KV_HEADS * DUP:].astype(BF16)

        @pl.when(i == n_tiles - 1)
        def _():
            kvtail_ref[0] = _dot(h2b[T - WINDOW:], wkv_ref[...]) + bkv_ref[...]


def _const_spec(shape):
    nd = len(shape)
    return pl.BlockSpec(shape, lambda b, i: (0,) * nd, pipeline_mode=pl.Buffered(1))


def _prompt_layer(x, *, kind, is_meta, emit_kv, tile, name, halo_in=None, gcar_in,
                  pool=None, attn=None, ln, ffn, kv=None):
    B, L, _ = x.shape
    n_tiles = L // tile
    nb = tile // BLOCK
    row_spec = lambda w: pl.BlockSpec((1, tile, w), lambda b, i: (b, i, 0))
    ins, specs = [x], [row_spec(D_MODEL)]

    def add_const(a):
        ins.append(a)
        specs.append(_const_spec(a.shape))

    if kind == "pool":
        add_const(halo_in)
    add_const(gcar_in)
    if kind == "pool":
        add_const(pool[0]); add_const(pool[1])
    else:
        kd, vd, kmeta, vmeta, mask, sinks, wq, bq, wao, bao = attn
        prev_spec = pl.BlockSpec((1, BLOCK, N_KV_HEADS * DUP),
                                 lambda b, i: (b, jnp.maximum(i * nb - 1, 0), 0))
        ins += [kd, vd, kd, vd]
        specs += [row_spec(N_KV_HEADS * DUP), row_spec(N_KV_HEADS * DUP), prev_spec, prev_spec]
        add_const(kmeta); add_const(vmeta); add_const(mask)
        ins.append(sinks)
        specs.append(pl.BlockSpec(memory_space=pltpu.SMEM))
        for a in (wq, bq, wao, bao):
            add_const(a)
    for a in ln:
        add_const(a)
    for a in ffn:
        add_const(a)
    if emit_kv:
        for a in kv:
            add_const(a)

    out_shape = [jax.ShapeDtypeStruct((B, L, D_MODEL), F32),
                 jax.ShapeDtypeStruct((B, SUBLANES, D_FF), F32)]
    out_specs = [row_spec(D_MODEL), pl.BlockSpec((1, SUBLANES, D_FF), lambda b, i: (b, 0, 0))]
    if emit_kv:
        out_shape += [jax.ShapeDtypeStruct((B, L, N_KV_HEADS * DUP), BF16),
                      jax.ShapeDtypeStruct((B, L, N_KV_HEADS * DUP), BF16),
                      jax.ShapeDtypeStruct((B, WINDOW, 2 * KV_DIM), F32)]
        out_specs += [row_spec(N_KV_HEADS * DUP), row_spec(N_KV_HEADS * DUP),
                      pl.BlockSpec((1, WINDOW, 2 * KV_DIM), lambda b, i: (b, 0, 0))]
    scratch = [pltpu.VMEM((SUBLANES, D_FF), F32)]
    if kind == "pool":
        scratch.append(pltpu.VMEM((MAX_POOL, D_MODEL), F32))
    else:
        scratch += [pltpu.VMEM((tile + BLOCK, N_KV_HEADS * DUP), BF16),
                    pltpu.VMEM((tile + BLOCK, N_KV_HEADS * DUP), BF16),
                    pltpu.VMEM((tile, D_MODEL), BF16),
                    pltpu.VMEM((tile, D_MODEL), BF16)]
    body = functools.partial(_prompt_kernel, kind=kind, is_meta=is_meta, emit_kv=emit_kv,
                             tile=tile, n_tiles=n_tiles)
    return pl.pallas_call(
        body,
        grid=(B, n_tiles),
        in_specs=specs,
        out_specs=out_specs,
        out_shape=out_shape,
        scratch_shapes=scratch,
        compiler_params=pltpu.CompilerParams(
            dimension_semantics=("arbitrary", "arbitrary"), vmem_limit_bytes=VMEM_LIMIT),
        name=name,
    )(*ins)


def _sample_kernel(*refs, kind, emit_kv, emit_q):
    it = iter(refs)
    x_ref = next(it)
    if kind == "pool":
        sp_ref, pw_ref, ps_ref = next(it), next(it), next(it)
    else:
        res_ref, wao_ref, bao_ref = next(it), next(it), next(it)
    lmg_ref, lmb_ref, lfg_ref, lfb_ref = (next(it) for _ in range(4))
    wi_ref, cw_ref, cb_ref, wo_ref = (next(it) for _ in range(4))
    sc_ref = next(it)
    if emit_kv:
        wkv_ref, bkv_ref = next(it), next(it)
    if emit_q:
        wqe_ref, bqe_ref = next(it), next(it)
    y_ref, g_ref = next(it), next(it)
    if emit_kv:
        kv_ref = next(it)
    if emit_q:
        qe_ref = next(it)

    x = x_ref[...]
    if kind == "pool":
        parts = []
        for gi, w in enumerate(POOL_WINDOWS):
            xg = x[:, gi * POOL_GC:(gi + 1) * POOL_GC]
            win = xg
            for k in range(1, w):
                c0 = (MAX_POOL - 1 - k) * D_MODEL + gi * POOL_GC
                win = win + sp_ref[:, c0:c0 + POOL_GC]
            dg = win * (1.0 / w) - xg
            parts.append(_dot(dg.astype(BF16), pw_ref[gi]))
        mix = jnp.concatenate(parts, axis=1) * ps_ref[...]
    else:
        mix = _dot(res_ref[...], wao_ref[...]) + bao_ref[...]

    h1 = _layer_norm(ALPHA * x + mix, lmg_ref[...], lmb_ref[...])
    h1b = h1.astype(BF16)

    def taps(c, cols, g):
        g_ref[:, cols] = g
        return g, sc_ref[:, D_FF + c * FF_CHUNK:D_FF + (c + 1) * FF_CHUNK], sc_ref[:, cols]

    f = _ffn_chunks(h1b, wi_ref, cw_ref, cb_ref, wo_ref, taps)
    h2 = _layer_norm(ALPHA * h1 + f, lfg_ref[...], lfb_ref[...])
    y_ref[...] = h2
    h2b = h2.astype(BF16)
    if emit_kv:
        kv_ref[...] = _dot(h2b, wkv_ref[...]) + bkv_ref[...]
    if emit_q:
        qe_ref[...] = (_dot(h2b, wqe_ref[...]) + bqe_ref[...]).astype(BF16)


def _sample_layer(x, *, kind, emit_kv, emit_q, name, mixer, ln, ffn, conv_state, kv=None, qexp=None):
    n = x.shape[0]
    ins = [x, *mixer, *ln, *ffn, conv_state]
    if emit_kv:
        ins += list(kv)
    if emit_q:
        ins += list(qexp)
    out_shape = [jax.ShapeDtypeStruct((n, D_MODEL), F32), jax.ShapeDtypeStruct((n, D_FF), F32)]
    if emit_kv:
        out_shape.append(jax.ShapeDtypeStruct((n, 2 * KV_DIM), F32))
    if emit_q:
        out_shape.append(jax.ShapeDtypeStruct((n, N_HEADS * KV_DIM), BF16))
    body = functools.partial(_sample_kernel, kind=kind, emit_kv=emit_kv, emit_q=emit_q)
    return pl.pallas_call(
        body,
        out_shape=out_shape,
        compiler_params=pltpu.CompilerParams(vmem_limit_bytes=VMEM_LIMIT),
        name=name,
    )(*ins)


def _sample_attn_kernel(q_ref, k_ref, v_ref, sink_ref, o_ref):
    q = q_ref[...]
    k = k_ref[...].astype(BF16)
    v = v_ref[...].astype(BF16)
    s = jnp.einsum("bhc,bjc->bhj", q, k, preferred_element_type=F32) * ATTN_SCALE
    sink = sink_ref[...][None]
    m = jnp.max(jnp.maximum(s, sink), axis=-1, keepdims=True)
    e = jnp.exp(s - m)
    den = jnp.sum(e, axis=-1, keepdims=True) + jnp.exp(sink - m)
    p = (e / den).astype(BF16)
    o_ref[...] = jnp.einsum("bhj,bjc->bhc", p, v, preferred_element_type=F32).astype(BF16)


def _sample_attn(qexp, k_new, v_new, sinkb, name):
    n = qexp.shape[0]
    G = SAMPLE_GROUP
    return pl.pallas_call(
        _sample_attn_kernel,
        grid=(n // G,),
        in_specs=[pl.BlockSpec((G, N_HEADS, KV_DIM), lambda i: (i, 0, 0)),
                  pl.BlockSpec((G, WINDOW, KV_DIM), lambda i: (i, 0, 0)),
                  pl.BlockSpec((G, WINDOW, KV_DIM), lambda i: (i, 0, 0)),
                  pl.BlockSpec((N_HEADS, WINDOW), lambda i: (0, 0))],
        out_specs=pl.BlockSpec((G, N_HEADS, KV_DIM), lambda i: (i, 0, 0)),
        out_shape=jax.ShapeDtypeStruct((n, N_HEADS, KV_DIM), BF16),
        compiler_params=pltpu.CompilerParams(dimension_semantics=("arbitrary",)),
        name=name,
    )(qexp, k_new, v_new, sinkb)


def _dup_heads(w):
    lead = w.shape[:-1]
    w = w.reshape(*lead, N_KV_HEADS, 1, HEAD_DIM)
    return jnp.broadcast_to(w, (*lead, N_KV_HEADS, 2, HEAD_DIM)).reshape(*lead, N_KV_HEADS * DUP)


def _head_onehot():
    return (jnp.arange(N_HEADS)[:, None] // GROUP == jnp.arange(N_KV_HEADS)[None, :]).astype(F32)


def _expand_q(w):
    lead = w.shape[:-1]
    w = w.reshape(*lead, N_HEADS, 1, HEAD_DIM) * _head_onehot()[:, :, None]
    return w.reshape(*lead, N_HEADS * KV_DIM)


def _expand_o(w):
    w = w.reshape(N_HEADS, 1, HEAD_DIM, D_MODEL) * _head_onehot()[:, :, None, None]
    return w.reshape(N_HEADS * KV_DIM, D_MODEL)


def _band_masks():
    i = jnp.arange(BLOCK)[:, None]
    j = jnp.arange(2 * BLOCK)[None, :]
    band = (j > i) & (j <= i + BLOCK)
    masks = jnp.stack([band, band & (j >= PAD_ROWS), band & (j >= BLOCK + PAD_ROWS)])
    return jnp.where(masks, 0.0, NEG).astype(F32)


def kernel(x_prompt, x_sample, state_pool, state_conv, state_k_win, state_v_win, meta_tokens, pool_w, pool_scale, w_kv, b_kv, attn_w_q, attn_b_q, attn_sinks, attn_w_o, attn_b_o, ffn_w_in, ffn_conv_w, ffn_conv_b, ffn_w_out, ln_mix_g, ln_mix_b, ln_ffn_g, ln_ffn_b):
    B, S, _ = x_prompt.shape
    n_dec = x_sample.shape[0]
    assert x_sample.shape[1] == 1 and S % ROW_TILE == 0 and n_dec % SAMPLE_GROUP == 0

    row = lambda a: a[:, None, :]
    wi_b, wo_b = ffn_w_in.astype(BF16), ffn_w_out.astype(BF16)
    cb = row(ffn_conv_b)
    lmg, lmb, lfg, lfb = row(ln_mix_g), row(ln_mix_b), row(ln_ffn_g), row(ln_ffn_b)
    pw_b, pscale = pool_w.astype(BF16), row(pool_scale)
    wkv_b, bkv = w_kv.astype(BF16), b_kv[None, :]
    wkvd_b = jnp.concatenate([_dup_heads(w_kv[:, :KV_DIM]), _dup_heads(w_kv[:, KV_DIM:])], axis=1).astype(BF16)
    bkvd = jnp.concatenate([_dup_heads(b_kv[:KV_DIM]), _dup_heads(b_kv[KV_DIM:])])[None, :]
    wq_b, bq = attn_w_q.astype(BF16), row(attn_b_q)
    wao_b, bao = attn_w_o.astype(BF16), row(attn_b_o)
    wqe_b, bqe = _expand_q(attn_w_q).astype(BF16), row(_expand_q(attn_b_q))
    waoe_b = jax.vmap(_expand_o)(attn_w_o).astype(BF16)
    sinkb = jnp.broadcast_to(attn_sinks[:, :, None], (attn_sinks.shape[0], N_HEADS, WINDOW))
    masks = _band_masks()

    zeros_halo = jnp.zeros((MAX_POOL, D_MODEL), F32)
    zeros_gcar = jnp.zeros((SUBLANES, D_FF), F32)
    zeros_kv = jnp.zeros((BLOCK, N_KV_HEADS * DUP), BF16)

    hm = jnp.concatenate([jnp.zeros((PAD_ROWS, D_MODEL), F32), meta_tokens.astype(F32)])[None]
    hp = x_prompt
    hs = x_sample.reshape(n_dec, D_MODEL)
    sp2 = state_pool.reshape(N_A, n_dec, POOL_STATE * D_MODEL)
    sc2 = state_conv.reshape(DEPTH, n_dec, (CONV_W - 1) * D_FF)

    new_pool_p, new_pool_s, new_conv_p, new_conv_s = [], [], [], []
    for l in range(DEPTH):
        ln = (lmg[l], lmb[l], lfg[l], lfb[l])
        ffn = (wi_b[l], ffn_conv_w[l], cb[l], wo_b[l])
        if l < N_A:
            emit_kv = l == N_A - 1
            kvw = (wkvd_b, bkvd, wkv_b, bkv) if emit_kv else None
            pool = (pw_b[l], pscale[l])
            new_pool_p.append(hp[:, S - POOL_STATE:])
            new_pool_s.append(jnp.concatenate([state_pool[l][:, 1:], hs[:, None, :]], axis=1))
            halo = hm[0, PAD_ROWS:]
            outs_m = _prompt_layer(hm, kind="pool", is_meta=True, emit_kv=emit_kv, tile=BLOCK,
                                   name=f"meta_l{l}", halo_in=zeros_halo, gcar_in=zeros_gcar,
                                   pool=pool, ln=ln, ffn=ffn, kv=kvw)
            outs_p = _prompt_layer(hp, kind="pool", is_meta=False, emit_kv=emit_kv, tile=ROW_TILE,
                                   name=f"prompt_l{l}", halo_in=halo, gcar_in=outs_m[1][0],
                                   pool=pool, ln=ln, ffn=ffn, kv=kvw)
            outs_s = _sample_layer(hs, kind="pool", emit_kv=emit_kv, emit_q=emit_kv, name=f"sample_l{l}",
                                   mixer=(sp2[l], pw_b[l], pscale[l]), ln=ln, ffn=ffn, conv_state=sc2[l],
                                   kv=(wkv_b, bkv), qexp=(wqe_b[0], bqe[0]))
            if emit_kv:
                kd_m, vd_m = outs_m[2][0], outs_m[3][0]
                kd_p, vd_p, kvtail = outs_p[2], outs_p[3], outs_p[4]
                kv_s, qe = outs_s[2], outs_s[3]
                new_k_s = jnp.concatenate([state_k_win[:, 1:], kv_s[:, None, :KV_DIM].reshape(n_dec, 1, N_KV_HEADS, HEAD_DIM)], axis=1)
                new_v_s = jnp.concatenate([state_v_win[:, 1:], kv_s[:, None, KV_DIM:].reshape(n_dec, 1, N_KV_HEADS, HEAD_DIM)], axis=1)
                k2 = new_k_s.reshape(n_dec, WINDOW, KV_DIM)
                v2 = new_v_s.reshape(n_dec, WINDOW, KV_DIM)
        else:
            bi = l - N_A
            attn_m = (kd_m[None], vd_m[None], zeros_kv, zeros_kv, masks, attn_sinks[bi], wq_b[bi], bq[bi], wao_b[bi], bao[bi])
            attn_p = (kd_p, vd_p, kd_m, vd_m, masks, attn_sinks[bi], wq_b[bi], bq[bi], wao_b[bi], bao[bi])
            outs_m = _prompt_layer(hm, kind="attn", is_meta=True, emit_kv=False, tile=BLOCK,
                                   name=f"meta_l{l}", gcar_in=zeros_gcar, attn=attn_m, ln=ln, ffn=ffn)
            outs_p = _prompt_layer(hp, kind="attn", is_meta=False, emit_kv=False, tile=ROW_TILE,
                                   name=f"prompt_l{l}", gcar_in=outs_m[1][0], attn=attn_p, ln=ln, ffn=ffn)
            res = _sample_attn(qe.reshape(n_dec, N_HEADS, KV_DIM), k2, v2, sinkb[bi], f"sample_attn_l{l}")
            emit_q = l + 1 < DEPTH
            outs_s = _sample_layer(hs, kind="attn", emit_kv=False, emit_q=emit_q, name=f"sample_l{l}",
                                   mixer=(res.reshape(n_dec, N_HEADS * KV_DIM), waoe_b[bi], bao[bi]),
                                   ln=ln, ffn=ffn, conv_state=sc2[l],
                                   qexp=(wqe_b[bi + 1], bqe[bi + 1]) if emit_q else None)
            if emit_q:
                qe = outs_s[2]
        new_conv_p.append(outs_p[1][:, SUBLANES - (CONV_W - 1):])
        new_conv_s.append(jnp.stack([state_conv[l][:, 1], outs_s[1]], axis=1))
        hm, hp, hs = outs_m[0], outs_p[0], outs_s[0]

    new_k_p = kvtail[:, :, :KV_DIM].reshape(B, WINDOW, N_KV_HEADS, HEAD_DIM)
    new_v_p = kvtail[:, :, KV_DIM:].reshape(B, WINDOW, N_KV_HEADS, HEAD_DIM)
    return (hp, hs.reshape(n_dec, 1, D_MODEL),
            jnp.stack(new_pool_p), jnp.stack(new_pool_s),
            jnp.stack(new_conv_p), jnp.stack(new_conv_s),
            new_k_p, new_v_p, new_k_s, new_v_s)
```

```python
import functools

import jax
import jax.numpy as jnp
from jax import lax
from jax.experimental import pallas as pl
from jax.experimental.pallas import tpu as pltpu

F32 = jnp.float32
BF16 = jnp.bfloat16

D_MODEL = 1024
DEPTH = 4
N_META = 16
N_A = DEPTH // 2
POOL_WINDOWS = (2, 4, 8, 16)
N_POOL_GROUPS = len(POOL_WINDOWS)
POOL_GC = D_MODEL // N_POOL_GROUPS
MAX_POOL = max(POOL_WINDOWS)
POOL_STATE = MAX_POOL - 1
HEAD_DIM = 64
N_HEADS = D_MODEL // HEAD_DIM
N_KV_HEADS = 4
GROUP = N_HEADS // N_KV_HEADS
KV_DIM = N_KV_HEADS * HEAD_DIM
WINDOW = 128
BLOCK = 128
ATTN_SCALE = HEAD_DIM ** -0.5
D_FF = 2816
CONV_W = 3
ALPHA = (2.0 * DEPTH) ** 0.25
LN_EPS = 1e-5
NEG = -1e30

SUBLANES = 8
LANES = 128
PAD_ROWS = BLOCK - N_META
FF_CHUNK = 256
N_FF_CHUNKS = D_FF // FF_CHUNK
ROW_TILE = 512
SAMPLE_GROUP = 16
DUP = 2 * HEAD_DIM
VMEM_LIMIT = 58 * 1024 * 1024


def _layer_norm(z, g, b):
    mu = jnp.mean(z, axis=-1, keepdims=True)
    zc = z - mu
    var = jnp.mean(zc * zc, axis=-1, keepdims=True)
    return zc * lax.rsqrt(var + LN_EPS) * g + b


def _silu(c):
    return c * (1.0 / (1.0 + jnp.exp(-c)))


def _dot(a, b):
    return jnp.dot(a, b, preferred_element_type=F32)


def _ffn_chunks(h1b, wi_ref, cw_ref, cb_ref, wo_ref, taps):
    def gate_up(c):
        return (_dot(h1b, wi_ref[:, c * FF_CHUNK:(c + 1) * FF_CHUNK]),
                _dot(h1b, wi_ref[:, D_FF + c * FF_CHUNK:D_FF + (c + 1) * FF_CHUNK]))

    acc = None
    nxt = gate_up(0)
    for c in range(N_FF_CHUNKS):
        cols = slice(c * FF_CHUNK, (c + 1) * FF_CHUNK)
        g, u = nxt
        if c + 1 < N_FF_CHUNKS:
            nxt = gate_up(c + 1)
        g, g1, g2 = taps(c, cols, g)
        cc = (cb_ref[:, cols] + g2 * cw_ref[0:1, cols] + g1 * cw_ref[1:2, cols]
              + g * cw_ref[2:3, cols])
        hh = (_silu(cc) * u).astype(BF16)
        part = _dot(hh, wo_ref[cols, :])
        acc = part if acc is None else acc + part
    return acc


def _prompt_kernel(*refs, kind, is_meta, emit_kv, tile, n_tiles, attn_idx):
    it = iter(refs)
    x_ref = next(it)
    halo_in_ref = next(it) if kind == "pool" else None
    gcar_in_ref = next(it)
    if kind == "pool":
        pw_ref, ps_ref = next(it), next(it)
    else:
        kcur_ref, vcur_ref, kprev_ref, vprev_ref, kmeta_ref, vmeta_ref = (next(it) for _ in range(6))
        mask_ref, sinks_ref = next(it), next(it)
        wq_ref, bq_ref, wao_ref, bao_ref = (next(it) for _ in range(4))
    lmg_ref, lmb_ref, lfg_ref, lfb_ref = (next(it) for _ in range(4))
    wi_ref, cw_ref, cb_ref, wo_ref = (next(it) for _ in range(4))
    if emit_kv:
        wkvd_ref, bkvd_ref, wkv_ref, bkv_ref = (next(it) for _ in range(4))
    y_ref, gtail_ref = next(it), next(it)
    if emit_kv:
        kd_ref, vd_ref, kvtail_ref = next(it), next(it), next(it)
    gcar_ref = next(it)
    if kind == "pool":
        halo_ref = next(it)
    else:
        kbuf_ref, vbuf_ref, qb_ref, ob_ref = (next(it) for _ in range(4))

    i = pl.program_id(1)
    T = tile
    x = x_ref[0]
    rows = lax.broadcasted_iota(jnp.int32, (T, 1), 0)
    live = rows >= PAD_ROWS

    @pl.when(i == 0)
    def _():
        gcar_ref[...] = gcar_in_ref[...]
        if kind == "pool":
            halo_ref[...] = halo_in_ref[...]

    if kind == "pool":
        if is_meta:
            x = jnp.where(live, x, 0.0)
        xe = jnp.concatenate([halo_ref[...], x], axis=0)
        s2 = xe + pltpu.roll(xe, 1, 0)
        s2r = s2[:, POOL_GC:]
        s4 = s2r + pltpu.roll(s2r, 2, 0)
        s4r = s4[:, POOL_GC:]
        s8 = s4r + pltpu.roll(s4r, 4, 0)
        s8r = s8[:, POOL_GC:]
        s16 = s8r + pltpu.roll(s8r, 8, 0)
        wins = (s2[MAX_POOL:, :POOL_GC], s4[MAX_POOL:, :POOL_GC],
                s8[MAX_POOL:, :POOL_GC], s16[MAX_POOL:])
        halo_ref[...] = x[T - MAX_POOL:, :]
        parts = []
        for gi, w in enumerate(POOL_WINDOWS):
            xg = x[:, gi * POOL_GC:(gi + 1) * POOL_GC]
            if is_meta:
                cnt = jnp.clip(rows - (PAD_ROWS - 1), 1, w).astype(F32)
                dg = wins[gi] / cnt - xg
            else:
                dg = wins[gi] * (1.0 / w) - xg
            parts.append(_dot(dg.astype(BF16), pw_ref[gi]))
        mix = jnp.concatenate(parts, axis=1) * ps_ref[...]
    else:
        nblk = T // BLOCK
        qb_ref[...] = (_dot(x.astype(BF16), wq_ref[...]) + bq_ref[...]).astype(BF16)

        @pl.when(i == 0)
        def _():
            kbuf_ref[0:BLOCK, :] = kmeta_ref[...]
            vbuf_ref[0:BLOCK, :] = vmeta_ref[...]

        @pl.when(i > 0)
        def _():
            kbuf_ref[0:BLOCK, :] = kprev_ref[0]
            vbuf_ref[0:BLOCK, :] = vprev_ref[0]

        kbuf_ref[BLOCK:, :] = kcur_ref[0]
        vbuf_ref[BLOCK:, :] = vcur_ref[0]
        lo = lax.broadcasted_iota(jnp.int32, (BLOCK, LANES), 1) < HEAD_DIM
        for j in range(nblk):
            r0 = j * BLOCK
            if is_meta:
                madd = mask_ref[2]
            elif j == 0:
                madd = mask_ref[jnp.where(i == 0, 1, 0)]
            else:
                madd = mask_ref[0]
            for h in range(N_KV_HEADS):
                c0 = h * GROUP * HEAD_DIM
                qa = qb_ref[r0:r0 + BLOCK, c0:c0 + LANES]
                qc = qb_ref[r0:r0 + BLOCK, c0 + LANES:c0 + 2 * LANES]
                z = jnp.zeros_like(qa)
                lhs = jnp.concatenate([jnp.where(lo, qa, z), jnp.where(lo, z, qa),
                                       jnp.where(lo, qc, z), jnp.where(lo, z, qc)], axis=0)
                kk = kbuf_ref[r0:r0 + 2 * BLOCK, h * DUP:(h + 1) * DUP]
                vv = vbuf_ref[r0:r0 + 2 * BLOCK, h * DUP:(h + 1) * DUP]
                s = lax.dot_general(lhs, kk, (((1,), (1,)), ((), ())),
                                    preferred_element_type=F32) * ATTN_SCALE
                ps = []
                for g in range(GROUP):
                    sg = s[g * BLOCK:(g + 1) * BLOCK] + madd
                    sink = sinks_ref[attn_idx, h * GROUP + g]
                    m = jnp.maximum(jnp.max(sg, axis=-1, keepdims=True), sink)
                    e = jnp.exp(sg - m)
                    den = jnp.sum(e, axis=-1, keepdims=True) + jnp.exp(sink - m)
                    ps.append((e / den).astype(BF16))
                res = _dot(jnp.concatenate(ps, axis=0), vv)
                oa = jnp.where(lo, res[0:BLOCK], res[BLOCK:2 * BLOCK])
                oc = jnp.where(lo, res[2 * BLOCK:3 * BLOCK], res[3 * BLOCK:])
                ob_ref[r0:r0 + BLOCK, c0:c0 + LANES] = oa.astype(BF16)
                ob_ref[r0:r0 + BLOCK, c0 + LANES:c0 + 2 * LANES] = oc.astype(BF16)
        mix = _dot(ob_ref[...], wao_ref[...]) + bao_ref[...]

    h1 = _layer_norm(ALPHA * x + mix, lmg_ref[...], lmb_ref[...])
    h1b = h1.astype(BF16)

    def taps(c, cols, g):
        if is_meta:
            g = jnp.where(live, g, 0.0)
        gext = jnp.concatenate([gcar_ref[:, cols], g], axis=0)
        g1 = pltpu.roll(gext, 1, 0)[SUBLANES:]
        g2 = pltpu.roll(gext, 2, 0)[SUBLANES:]
        gcar_ref[:, cols] = g[T - SUBLANES:]
        return g, g1, g2

    f = _ffn_chunks(h1b, wi_ref, cw_ref, cb_ref, wo_ref, taps)
    h2 = _layer_norm(ALPHA * h1 + f, lfg_ref[...], lfb_ref[...])
    y_ref[0] = h2
    gtail_ref[0] = gcar_ref[...]

    if emit_kv:
        h2b = h2.astype(BF16)
        kvd = _dot(h2b, wkvd_ref[...]) + bkvd_ref[...]
        kd_ref[0] = kvd[:, :N_KV_HEADS * DUP].astype(BF16)
        vd_ref[0] = kvd[:, N_KV_HEADS * DUP:].astype(BF16)

        @pl.when(i == n_tiles - 1)
        def _():
            kvtail_ref[0] = _dot(h2b[T - WINDOW:], wkv_ref[...]) + bkv_ref[...]


def _const_spec(a):
    nd = a.ndim
    return pl.BlockSpec(a.shape, lambda *_: (0,) * nd, pipeline_mode=pl.Buffered(1))


def _layer_spec(a, l):
    nd = a.ndim
    return pl.BlockSpec((None,) + a.shape[1:], lambda *_: (l,) + (0,) * (nd - 1),
                        pipeline_mode=pl.Buffered(1))


def _prompt_layer(x, *, layer, kind, is_meta, emit_kv, tile, name, halo_in=None, gcar_in,
                  pool=None, attn=None, ln, ffn, kv=None):
    B, L, _ = x.shape
    n_tiles = L // tile
    nb = tile // BLOCK
    attn_idx = layer - N_A
    row_spec = lambda w: pl.BlockSpec((1, tile, w), lambda b, i: (b, i, 0))
    ins, specs = [x], [row_spec(D_MODEL)]

    def add(a, spec):
        ins.append(a)
        specs.append(spec)

    if kind == "pool":
        add(halo_in, _const_spec(halo_in))
    add(gcar_in, _const_spec(gcar_in))
    if kind == "pool":
        for a in pool:
            add(a, _layer_spec(a, layer))
    else:
        kd, vd, kmeta, vmeta, mask, sinks, wq, bq, wao, bao = attn
        prev_spec = pl.BlockSpec((1, BLOCK, N_KV_HEADS * DUP),
                                 lambda b, i: (b, jnp.maximum(i * nb - 1, 0), 0))
        ins += [kd, vd, kd, vd]
        specs += [row_spec(N_KV_HEADS * DUP), row_spec(N_KV_HEADS * DUP), prev_spec, prev_spec]
        for a in (kmeta, vmeta, mask):
            add(a, _const_spec(a))
        add(sinks, pl.BlockSpec(memory_space=pltpu.SMEM))
        for a in (wq, bq, wao, bao):
            add(a, _layer_spec(a, attn_idx))
    for a in (*ln, *ffn):
        add(a, _layer_spec(a, layer))
    if emit_kv:
        for a in kv:
            add(a, _const_spec(a))

    out_shape = [jax.ShapeDtypeStruct((B, L, D_MODEL), F32),
                 jax.ShapeDtypeStruct((B, SUBLANES, D_FF), F32)]
    out_specs = [row_spec(D_MODEL), pl.BlockSpec((1, SUBLANES, D_FF), lambda b, i: (b, 0, 0))]
    if emit_kv:
        out_shape += [jax.ShapeDtypeStruct((B, L, N_KV_HEADS * DUP), BF16),
                      jax.ShapeDtypeStruct((B, L, N_KV_HEADS * DUP), BF16),
                      jax.ShapeDtypeStruct((B, WINDOW, 2 * KV_DIM), F32)]
        out_specs += [row_spec(N_KV_HEADS * DUP), row_spec(N_KV_HEADS * DUP),
                      pl.BlockSpec((1, WINDOW, 2 * KV_DIM), lambda b, i: (b, 0, 0))]
    scratch = [pltpu.VMEM((SUBLANES, D_FF), F32)]
    if kind == "pool":
        scratch.append(pltpu.VMEM((MAX_POOL, D_MODEL), F32))
    else:
        scratch += [pltpu.VMEM((tile + BLOCK, N_KV_HEADS * DUP), BF16),
                    pltpu.VMEM((tile + BLOCK, N_KV_HEADS * DUP), BF16),
                    pltpu.VMEM((tile, D_MODEL), BF16),
                    pltpu.VMEM((tile, D_MODEL), BF16)]
    body = functools.partial(_prompt_kernel, kind=kind, is_meta=is_meta, emit_kv=emit_kv,
                             tile=tile, n_tiles=n_tiles, attn_idx=attn_idx)
    return pl.pallas_call(
        body,
        grid=(B, n_tiles),
        in_specs=specs,
        out_specs=out_specs,
        out_shape=out_shape,
        scratch_shapes=scratch,
        compiler_params=pltpu.CompilerParams(
            dimension_semantics=("arbitrary", "arbitrary"), vmem_limit_bytes=VMEM_LIMIT),
        name=name,
    )(*ins)


def _sample_kernel(*refs, kind, emit_kv, emit_q):
    it = iter(refs)
    x_ref = next(it)
    if kind == "pool":
        sp_ref, pw_ref, ps_ref = next(it), next(it), next(it)
    else:
        o_ref, wao_ref, bao_ref = next(it), next(it), next(it)
    lmg_ref, lmb_ref, lfg_ref, lfb_ref = (next(it) for _ in range(4))
    wg_ref, wu_ref, cw_ref, cb_ref, wo_ref, s0_ref, s1_ref = (next(it) for _ in range(7))
    if emit_kv:
        wkv_ref, bkv_ref = next(it), next(it)
    if emit_q:
        wq_ref, bq_ref = next(it), next(it)
    y_ref, g_ref = next(it), next(it)
    if emit_kv:
        kv_ref = next(it)
    if emit_q:
        q_ref = next(it)
    h1_ref, h1b_ref, acc_ref = next(it), next(it), next(it)

    c = pl.program_id(0)

    @pl.when(c == 0)
    def _():
        x = x_ref[...]
        if kind == "pool":
            parts = []
            for gi, w in enumerate(POOL_WINDOWS):
                cols = slice(gi * POOL_GC, (gi + 1) * POOL_GC)
                win = x[:, cols]
                for k in range(1, w):
                    win = win + sp_ref[POOL_STATE - k, :, cols]
                dg = win * (1.0 / w) - x[:, cols]
                parts.append(_dot(dg.astype(BF16), pw_ref[gi]))
            mix = jnp.concatenate(parts, axis=1) * ps_ref[...]
        else:
            mix = _dot(o_ref[...], wao_ref[...]) + bao_ref[...]
        h1 = _layer_norm(ALPHA * x + mix, lmg_ref[...], lmb_ref[...])
        h1_ref[...] = h1
        h1b_ref[...] = h1.astype(BF16)
        acc_ref[...] = jnp.zeros_like(acc_ref)

    h1b = h1b_ref[...]
    g = _dot(h1b, wg_ref[...])
    u = _dot(h1b, wu_ref[...])
    g_ref[...] = g
    cc = cb_ref[...] + s0_ref[...] * cw_ref[0:1, :] + s1_ref[...] * cw_ref[1:2, :] + g * cw_ref[2:3, :]
    acc_ref[...] += _dot((_silu(cc) * u).astype(BF16), wo_ref[...])

    @pl.when(c == N_FF_CHUNKS - 1)
    def _():
        h2 = _layer_norm(ALPHA * h1_ref[...] + acc_ref[...], lfg_ref[...], lfb_ref[...])
        y_ref[...] = h2
        h2b = h2.astype(BF16)
        if emit_kv:
            kv_ref[...] = _dot(h2b, wkv_ref[...]) + bkv_ref[...]
        if emit_q:
            q_ref[...] = _dot(h2b, wq_ref[...]) + bq_ref[...]


def _sample_layer(x, *, layer, kind, emit_kv, emit_q, name, mixer, ln, ffn, conv_t, kv=None, qproj=None):
    n = x.shape[0]
    ins, specs = [x], [_const_spec(x)]

    def add(a, spec):
        ins.append(a)
        specs.append(spec)

    if kind == "pool":
        for a in mixer:
            add(a, _layer_spec(a, layer))
    else:
        o, wao, bao = mixer
        add(o, _const_spec(o))
        add(wao, _layer_spec(wao, layer - N_A))
        add(bao, _layer_spec(bao, layer - N_A))
    for a in ln:
        add(a, _layer_spec(a, layer))
    wi, cw, cb, wo = ffn
    add(wi, pl.BlockSpec((None, D_MODEL, FF_CHUNK), lambda c: (layer, 0, c)))
    add(wi, pl.BlockSpec((None, D_MODEL, FF_CHUNK), lambda c: (layer, 0, N_FF_CHUNKS + c)))
    add(cw, pl.BlockSpec((None, CONV_W, FF_CHUNK), lambda c: (layer, 0, c)))
    add(cb, pl.BlockSpec((None, 1, FF_CHUNK), lambda c: (layer, 0, c)))
    add(wo, pl.BlockSpec((None, FF_CHUNK, D_MODEL), lambda c: (layer, c, 0)))
    add(conv_t, pl.BlockSpec((None, None, n, FF_CHUNK), lambda c: (layer, 0, 0, c)))
    add(conv_t, pl.BlockSpec((None, None, n, FF_CHUNK), lambda c: (layer, 1, 0, c)))
    if emit_kv:
        for a in kv:
            add(a, _const_spec(a))
    if emit_q:
        for a in qproj:
            add(a, _layer_spec(a, layer + 1 - N_A))

    full = lambda w: pl.BlockSpec((n, w), lambda c: (0, 0))
    out_shape = [jax.ShapeDtypeStruct((n, D_MODEL), F32), jax.ShapeDtypeStruct((n, D_FF), F32)]
    out_specs = [full(D_MODEL), pl.BlockSpec((n, FF_CHUNK), lambda c: (0, c))]
    if emit_kv:
        out_shape.append(jax.ShapeDtypeStruct((n, 2 * KV_DIM), F32))
        out_specs.append(full(2 * KV_DIM))
    if emit_q:
        out_shape.append(jax.ShapeDtypeStruct((n, D_MODEL), F32))
        out_specs.append(full(D_MODEL))
    body = functools.partial(_sample_kernel, kind=kind, emit_kv=emit_kv, emit_q=emit_q)
    return pl.pallas_call(
        body,
        grid=(N_FF_CHUNKS,),
        in_specs=specs,
        out_specs=out_specs,
        out_shape=out_shape,
        scratch_shapes=[pltpu.VMEM((n, D_MODEL), F32), pltpu.VMEM((n, D_MODEL), BF16),
                        pltpu.VMEM((n, D_MODEL), F32)],
        compiler_params=pltpu.CompilerParams(dimension_semantics=("arbitrary",),
                                             vmem_limit_bytes=VMEM_LIMIT),
        name=name,
    )(*ins)


def _sample_attn_kernel(*refs, update):
    it = iter(refs)
    q_ref, kt_ref, vt_ref = next(it), next(it), next(it)
    if update:
        kn_ref, vn_ref = next(it), next(it)
    sink_ref, o_ref = next(it), next(it)
    if update:
        nkt_ref, nvt_ref = next(it), next(it)

    kt, vt = kt_ref[...], vt_ref[...]
    if update:
        newest = lax.broadcasted_iota(jnp.int32, (KV_DIM, WINDOW), 1) == WINDOW - 1

        def advance(win, new_ref):
            return jnp.stack([jnp.where(newest, new_ref[:, b:b + 1], pltpu.roll(win[b], WINDOW - 1, 1))
                              for b in range(SAMPLE_GROUP)])

        kt, vt = advance(kt, kn_ref), advance(vt, vn_ref)
        nkt_ref[...] = kt
        nvt_ref[...] = vt
    q = q_ref[...]
    s = jnp.einsum("bhc,bcj->bhj", q, kt.astype(BF16), preferred_element_type=F32) * ATTN_SCALE
    sink = sink_ref[...][None]
    m = jnp.max(jnp.maximum(s, sink), axis=-1, keepdims=True)
    e = jnp.exp(s - m)
    den = jnp.sum(e, axis=-1, keepdims=True) + jnp.exp(sink - m)
    p = (e / den).astype(BF16)
    o_ref[...] = jnp.einsum("bhj,bcj->bhc", p, vt.astype(BF16), preferred_element_type=F32).astype(BF16)


def _sample_attn(qexp, kt, vt, sinkb, layer, name, new_cols=None):
    n = qexp.shape[0]
    G = SAMPLE_GROUP
    update = new_cols is not None
    win_spec = pl.BlockSpec((G, KV_DIM, WINDOW), lambda i: (i, 0, 0))
    q_spec = pl.BlockSpec((G, N_HEADS, KV_DIM), lambda i: (i, 0, 0))
    ins, specs = [qexp, kt, vt], [q_spec, win_spec, win_spec]
    if update:
        col_spec = pl.BlockSpec((None, KV_DIM, G), lambda i: (i, 0, 0))
        ins += list(new_cols)
        specs += [col_spec, col_spec]
    ins.append(sinkb)
    specs.append(pl.BlockSpec((None, N_HEADS, WINDOW), lambda i: (layer - N_A, 0, 0)))
    out_shape = [jax.ShapeDtypeStruct((n, N_HEADS, KV_DIM), BF16)]
    out_specs = [q_spec]
    if update:
        out_shape += [jax.ShapeDtypeStruct(kt.shape, F32)] * 2
        out_specs += [win_spec, win_spec]
    return pl.pallas_call(
        functools.partial(_sample_attn_kernel, update=update),
        grid=(n // G,),
        in_specs=specs,
        out_specs=out_specs,
        out_shape=out_shape,
        compiler_params=pltpu.CompilerParams(dimension_semantics=("arbitrary",)),
        name=name,
    )(*ins)


def _dup_heads(w):
    lead = w.shape[:-1]
    w = w.reshape(*lead, N_KV_HEADS, 1, HEAD_DIM)
    return jnp.broadcast_to(w, (*lead, N_KV_HEADS, 2, HEAD_DIM)).reshape(*lead, N_KV_HEADS * DUP)


def _expand_q(q):
    n = q.shape[0]
    q = q.reshape(n, N_KV_HEADS, GROUP, HEAD_DIM)
    parts = [jnp.pad(q[:, h], ((0, 0), (0, 0), (h * HEAD_DIM, KV_DIM - (h + 1) * HEAD_DIM)))
             for h in range(N_KV_HEADS)]
    return jnp.stack(parts, axis=1).reshape(n, N_HEADS, KV_DIM)


def _contract_o(res):
    n = res.shape[0]
    res = res.reshape(n, N_KV_HEADS, GROUP, N_KV_HEADS, HEAD_DIM)
    return jnp.stack([res[:, h, :, h, :] for h in range(N_KV_HEADS)], axis=1).reshape(n, N_HEADS * HEAD_DIM)


def _band_masks():
    i = jnp.arange(BLOCK)[:, None]
    j = jnp.arange(2 * BLOCK)[None, :]
    band = (j > i) & (j <= i + BLOCK)
    masks = jnp.stack([band, band & (j >= PAD_ROWS), band & (j >= BLOCK + PAD_ROWS)])
    return jnp.where(masks, 0.0, NEG).astype(F32)


def kernel(x_prompt, x_sample, state_pool, state_conv, state_k_win, state_v_win, meta_tokens, pool_w, pool_scale, w_kv, b_kv, attn_w_q, attn_b_q, attn_sinks, attn_w_o, attn_b_o, ffn_w_in, ffn_conv_w, ffn_conv_b, ffn_w_out, ln_mix_g, ln_mix_b, ln_ffn_g, ln_ffn_b):
    B, S, _ = x_prompt.shape
    n_dec = x_sample.shape[0]
    assert x_sample.shape[1] == 1 and S % ROW_TILE == 0 and n_dec % SAMPLE_GROUP == 0

    row = lambda a: a[:, None, :]
    ffn = (ffn_w_in.astype(BF16), ffn_conv_w, row(ffn_conv_b), ffn_w_out.astype(BF16))
    ln = (row(ln_mix_g), row(ln_mix_b), row(ln_ffn_g), row(ln_ffn_b))
    pool = (pool_w.astype(BF16), row(pool_scale))
    wkv_b, bkv = w_kv.astype(BF16), b_kv[None, :]
    wkvd_b = jnp.concatenate([_dup_heads(w_kv[:, :KV_DIM]), _dup_heads(w_kv[:, KV_DIM:])], axis=1).astype(BF16)
    bkvd = jnp.concatenate([_dup_heads(b_kv[:KV_DIM]), _dup_heads(b_kv[KV_DIM:])])[None, :]
    kvw = (wkvd_b, bkvd, wkv_b, bkv)
    wq_b, bq = attn_w_q.astype(BF16), row(attn_b_q)
    wao_b, bao = attn_w_o.astype(BF16), row(attn_b_o)
    sinkb = jnp.broadcast_to(attn_sinks[:, :, None], (attn_sinks.shape[0], N_HEADS, WINDOW))
    masks = _band_masks()

    zeros_halo = jnp.zeros((MAX_POOL, D_MODEL), F32)
    zeros_gcar = jnp.zeros((SUBLANES, D_FF), F32)
    zeros_kv = jnp.zeros((BLOCK, N_KV_HEADS * DUP), BF16)

    hm = jnp.concatenate([jnp.zeros((PAD_ROWS, D_MODEL), F32), meta_tokens.astype(F32)])[None]
    hp = x_prompt
    hs = x_sample.reshape(n_dec, D_MODEL)
    pool_t = jnp.transpose(state_pool, (0, 2, 1, 3))
    conv_t = jnp.transpose(state_conv, (0, 2, 1, 3))
    kt = jnp.transpose(state_k_win, (0, 2, 3, 1)).reshape(n_dec, KV_DIM, WINDOW)
    vt = jnp.transpose(state_v_win, (0, 2, 3, 1)).reshape(n_dec, KV_DIM, WINDOW)

    new_pool_p, new_pool_t, new_conv_p, new_conv_t = [], [], [], []
    for l in range(DEPTH):
        if l < N_A:
            emit_kv = l == N_A - 1
            new_pool_p.append(hp[:, S - POOL_STATE:])
            new_pool_t.append(jnp.concatenate([pool_t[l, 1:], hs[None]], axis=0))
            outs_m = _prompt_layer(hm, layer=l, kind="pool", is_meta=True, emit_kv=emit_kv, tile=BLOCK,
                                   name=f"meta_l{l}", halo_in=zeros_halo, gcar_in=zeros_gcar,
                                   pool=pool, ln=ln, ffn=ffn, kv=kvw)
            outs_p = _prompt_layer(hp, layer=l, kind="pool", is_meta=False, emit_kv=emit_kv, tile=ROW_TILE,
                                   name=f"prompt_l{l}", halo_in=hm[0, PAD_ROWS:], gcar_in=outs_m[1][0],
                                   pool=pool, ln=ln, ffn=ffn, kv=kvw)
            outs_s = _sample_layer(hs, layer=l, kind="pool", emit_kv=emit_kv, emit_q=emit_kv,
                                   name=f"sample_l{l}", mixer=(pool_t, *pool), ln=ln, ffn=ffn, conv_t=conv_t,
                                   kv=(wkv_b, bkv), qproj=(wq_b, bq))
            if emit_kv:
                kd_m, vd_m = outs_m[2][0], outs_m[3][0]
                kd_p, vd_p, kvtail = outs_p[2], outs_p[3], outs_p[4]
                kv_s, q_s = outs_s[2], outs_s[3]
        else:
            attn_w = (masks, attn_sinks, wq_b, bq, wao_b, bao)
            outs_m = _prompt_layer(hm, layer=l, kind="attn", is_meta=True, emit_kv=False, tile=BLOCK,
                                   name=f"meta_l{l}", gcar_in=zeros_gcar,
                                   attn=(kd_m[None], vd_m[None], zeros_kv, zeros_kv, *attn_w), ln=ln, ffn=ffn)
            outs_p = _prompt_layer(hp, layer=l, kind="attn", is_meta=False, emit_kv=False, tile=ROW_TILE,
                                   name=f"prompt_l{l}", gcar_in=outs_m[1][0],
                                   attn=(kd_p, vd_p, kd_m, vd_m, *attn_w), ln=ln, ffn=ffn)
            qexp = _expand_q(q_s.astype(BF16))
            if l == N_A:
                cols = lambda a: jnp.transpose(a.reshape(n_dec // SAMPLE_GROUP, SAMPLE_GROUP, KV_DIM), (0, 2, 1))
                res, kt, vt = _sample_attn(qexp, kt, vt, sinkb, l, f"sample_attn_l{l}",
                                           new_cols=(cols(kv_s[:, :KV_DIM]), cols(kv_s[:, KV_DIM:])))
            else:
                res = _sample_attn(qexp, kt, vt, sinkb, l, f"sample_attn_l{l}")[0]
            emit_q = l + 1 < DEPTH
            outs_s = _sample_layer(hs, layer=l, kind="attn", emit_kv=False, emit_q=emit_q, name=f"sample_l{l}",
                                   mixer=(_contract_o(res), wao_b, bao), ln=ln, ffn=ffn, conv_t=conv_t,
                                   qproj=(wq_b, bq))
            if emit_q:
                q_s = outs_s[2]
        new_conv_p.append(outs_p[1][:, SUBLANES - (CONV_W - 1):])
        new_conv_t.append(jnp.stack([conv_t[l, 1], outs_s[1]]))
        hm, hp, hs = outs_m[0], outs_p[0], outs_s[0]

    unwin = lambda t: jnp.transpose(t.reshape(-1, N_KV_HEADS, HEAD_DIM, WINDOW), (0, 3, 1, 2))
    new_k_p = kvtail[:, :, :KV_DIM].reshape(B, WINDOW, N_KV_HEADS, HEAD_DIM)
    new_v_p = kvtail[:, :, KV_DIM:].reshape(B, WINDOW, N_KV_HEADS, HEAD_DIM)
    return (hp, hs.reshape(n_dec, 1, D_MODEL),
            jnp.stack(new_pool_p), jnp.transpose(jnp.stack(new_pool_t), (0, 2, 1, 3)),
            jnp.stack(new_conv_p), jnp.transpose(jnp.stack(new_conv_t), (0, 2, 1, 3)),
            new_k_p, new_v_p, unwin(kt), unwin(vt))
```

```python
import functools

import jax
import jax.numpy as jnp
from jax import lax
from jax.experimental import pallas as pl
from jax.experimental.pallas import tpu as pltpu

F32 = jnp.float32
BF16 = jnp.bfloat16

D_MODEL = 1024
DEPTH = 4
N_META = 16
N_A = DEPTH // 2
POOL_WINDOWS = (2, 4, 8, 16)
N_POOL_GROUPS = len(POOL_WINDOWS)
POOL_GC = D_MODEL // N_POOL_GROUPS
MAX_POOL = max(POOL_WINDOWS)
POOL_STATE = MAX_POOL - 1
HEAD_DIM = 64
N_HEADS = D_MODEL // HEAD_DIM
N_KV_HEADS = 4
GROUP = N_HEADS // N_KV_HEADS
KV_DIM = N_KV_HEADS * HEAD_DIM
WINDOW = 128
BLOCK = 128
ATTN_SCALE = HEAD_DIM ** -0.5
D_FF = 2816
CONV_W = 3
ALPHA = (2.0 * DEPTH) ** 0.25
LN_EPS = 1e-5
NEG = -1e30

SUBLANES = 8
LANES = 128
PAD_ROWS = BLOCK - N_META
FF_CHUNK = 256
N_FF_CHUNKS = D_FF // FF_CHUNK
OUT_BLOCK = 256
N_OUT_BLOCKS = D_MODEL // OUT_BLOCK
ROW_TILE = 512
SAMPLE_GROUP = 16
DUP = 2 * HEAD_DIM
KV_DUP = N_KV_HEADS * DUP
VMEM_LIMIT = 58 * 1024 * 1024


def _layer_norm(z, g, b):
    mu = jnp.mean(z, axis=-1, keepdims=True)
    zc = z - mu
    var = jnp.mean(zc * zc, axis=-1, keepdims=True)
    return zc * lax.rsqrt(var + LN_EPS) * g + b


def _silu(c):
    return c * (1.0 / (1.0 + jnp.exp(-c)))


def _dot(a, b):
    return jnp.dot(a, b, preferred_element_type=F32)


def _const_spec(a):
    nd = a.ndim
    return pl.BlockSpec(a.shape, lambda *_: (0,) * nd, pipeline_mode=pl.Buffered(1))


def _layer_spec(a, l):
    nd = a.ndim
    return pl.BlockSpec((None,) + a.shape[1:], lambda *_: (l,) + (0,) * (nd - 1),
                        pipeline_mode=pl.Buffered(1))


def _prompt_kernel(*refs, kind, is_meta, emit_kv, tile, n_tiles, n_steps, attn_idx):
    it = iter(refs)
    x_ref = next(it)
    halo_in_ref = next(it) if kind == "pool" else None
    gcar_in_ref = next(it)
    if kind == "pool":
        pw_ref, ps_ref = next(it), next(it)
    else:
        kcur_ref, vcur_ref, kprev_ref, vprev_ref, kmeta_ref, vmeta_ref = (next(it) for _ in range(6))
        mask_ref, sinks_ref = next(it), next(it)
        wq_ref, bq_ref, wao_ref, bao_ref = (next(it) for _ in range(4))
    lmg_ref, lmb_ref, lfg_ref, lfb_ref = (next(it) for _ in range(4))
    wi_ref, cw_ref, cb_ref, wo_ref = (next(it) for _ in range(4))
    if emit_kv:
        wkvd_ref, bkvd_ref, wkv_ref, bkv_ref = (next(it) for _ in range(4))
    y_ref, gtail_ref = next(it), next(it)
    if emit_kv:
        kd_ref, vd_ref, kvtail_ref = next(it), next(it), next(it)
    gcar_ref, hh_ref, h1_even_ref, h1_odd_ref, hb_ref = (next(it) for _ in range(5))
    if kind == "pool":
        halo_ref = next(it)
    else:
        qb_ref, ob_ref = next(it), next(it)
    if emit_kv:
        h2b_ref = next(it)

    T = tile
    n_rb = T // BLOCK
    s = pl.program_id(0)
    i = s % n_tiles
    has_cur = s < n_steps - 1

    @pl.when(s == 0)
    def _():
        hh_ref[...] = jnp.zeros_like(hh_ref)
        h1_even_ref[...] = jnp.zeros_like(h1_even_ref)
        h1_odd_ref[...] = jnp.zeros_like(h1_odd_ref)

    @pl.when(jnp.logical_and(has_cur, i == 0))
    def _():
        gcar_ref[...] = gcar_in_ref[...]
        if kind == "pool":
            halo_ref[...] = halo_in_ref[...]

    def block_rows(r):
        return slice(r * BLOCK, (r + 1) * BLOCK)

    def live_rows(r):
        return lax.broadcasted_iota(jnp.int32, (BLOCK, 1), 0) + r * BLOCK >= PAD_ROWS

    def down_block(n):
        return _dot(hh_ref[...], wo_ref[:, n * OUT_BLOCK:(n + 1) * OUT_BLOCK])

    def finish_rows(r, f_blocks, h1_old_ref):
        rows = block_rows(r)
        f = jnp.concatenate([fb[rows] for fb in f_blocks], axis=1)
        h2 = _layer_norm(ALPHA * h1_old_ref[rows, :] + f, lfg_ref[...], lfb_ref[...])
        y_ref[0, rows, :] = h2
        if emit_kv:
            h2b_ref[rows, :] = h2.astype(BF16)

    def kv_block(n):
        cols = slice(n * OUT_BLOCK, (n + 1) * OUT_BLOCK)
        kvd = (_dot(h2b_ref[...], wkvd_ref[:, cols]) + bkvd_ref[:, cols]).astype(BF16)
        if n * OUT_BLOCK < KV_DUP:
            kd_ref[0, :, cols] = kvd
        else:
            vd_ref[0, :, n * OUT_BLOCK - KV_DUP:(n + 1) * OUT_BLOCK - KV_DUP] = kvd

    def first_norm(r, x_blk, mix, h1_new_ref):
        rows = block_rows(r)
        h1 = _layer_norm(ALPHA * x_blk + mix, lmg_ref[...], lmb_ref[...])
        h1_new_ref[rows, :] = h1
        hb_ref[rows, :] = h1.astype(BF16)

    def pool_rows(r, h1_new_ref):
        rows = block_rows(r)
        x_blk = x_ref[0, rows, :]
        if is_meta:
            x_blk = jnp.where(live_rows(r), x_blk, 0.0)
        if r == 0:
            xe = jnp.concatenate([halo_ref[...], x_blk], axis=0)
        else:
            xe = x_ref[0, r * BLOCK - MAX_POOL:(r + 1) * BLOCK, :]
        s2 = xe + pltpu.roll(xe, 1, 0)
        s2r = s2[:, POOL_GC:]
        s4 = s2r + pltpu.roll(s2r, 2, 0)
        s4r = s4[:, POOL_GC:]
        s8 = s4r + pltpu.roll(s4r, 4, 0)
        s8r = s8[:, POOL_GC:]
        s16 = s8r + pltpu.roll(s8r, 8, 0)
        wins = (s2[MAX_POOL:, :POOL_GC], s4[MAX_POOL:, :POOL_GC],
                s8[MAX_POOL:, :POOL_GC], s16[MAX_POOL:])
        parts = []
        for gi, w in enumerate(POOL_WINDOWS):
            xg = x_blk[:, gi * POOL_GC:(gi + 1) * POOL_GC]
            if is_meta:
                t1 = lax.broadcasted_iota(jnp.int32, (BLOCK, 1), 0) + (r * BLOCK - PAD_ROWS + 1)
                dg = wins[gi] / jnp.clip(t1, 1, w).astype(F32) - xg
            else:
                dg = wins[gi] * (1.0 / w) - xg
            parts.append(_dot(dg.astype(BF16), pw_ref[gi]))
        mix = jnp.concatenate(parts, axis=1) * ps_ref[...]
        first_norm(r, x_blk, mix, h1_new_ref)

    def q_rows(r):
        rows = block_rows(r)
        qb_ref[rows, :] = (_dot(x_ref[0, rows, :].astype(BF16), wq_ref[...]) + bq_ref[...]).astype(BF16)

    lo = lax.broadcasted_iota(jnp.int32, (BLOCK, LANES), 1) < HEAD_DIM

    def keys_values(j, h):
        cols = slice(h * DUP, (h + 1) * DUP)
        if j == 0:
            first = i == 0
            kp = jnp.where(first, kmeta_ref[:, cols], kprev_ref[0, :, cols])
            vp = jnp.where(first, vmeta_ref[:, cols], vprev_ref[0, :, cols])
            return (jnp.concatenate([kp, kcur_ref[0, 0:BLOCK, cols]], axis=0),
                    jnp.concatenate([vp, vcur_ref[0, 0:BLOCK, cols]], axis=0))
        return (kcur_ref[0, (j - 1) * BLOCK:(j + 1) * BLOCK, cols],
                vcur_ref[0, (j - 1) * BLOCK:(j + 1) * BLOCK, cols])

    def scores(j, h):
        rows = block_rows(j)
        c0 = h * GROUP * HEAD_DIM
        qa = qb_ref[rows, c0:c0 + LANES]
        qc = qb_ref[rows, c0 + LANES:c0 + 2 * LANES]
        z = jnp.zeros_like(qa)
        lhs = jnp.concatenate([jnp.where(lo, qa, z), jnp.where(lo, z, qa),
                               jnp.where(lo, qc, z), jnp.where(lo, z, qc)], axis=0)
        kk, vv = keys_values(j, h)
        sc = lax.dot_general(lhs, kk, (((1,), (1,)), ((), ())), preferred_element_type=F32)
        return sc * ATTN_SCALE, vv

    def softmax(j, h, sc):
        if is_meta:
            madd = mask_ref[2]
        elif j == 0:
            madd = mask_ref[jnp.where(i == 0, 1, 0)]
        else:
            madd = mask_ref[0]
        ps = []
        for g in range(GROUP):
            sg = sc[g * BLOCK:(g + 1) * BLOCK] + madd
            sink = sinks_ref[attn_idx, h * GROUP + g]
            m = jnp.maximum(jnp.max(sg, axis=-1, keepdims=True), sink)
            e = jnp.exp(sg - m)
            den = jnp.sum(e, axis=-1, keepdims=True) + jnp.exp(sink - m)
            ps.append((e / den).astype(BF16))
        return jnp.concatenate(ps, axis=0)

    def weighted_values(j, h, p, vv):
        rows = block_rows(j)
        c0 = h * GROUP * HEAD_DIM
        res = _dot(p, vv)
        oa = jnp.where(lo, res[0:BLOCK], res[BLOCK:2 * BLOCK])
        oc = jnp.where(lo, res[2 * BLOCK:3 * BLOCK], res[3 * BLOCK:])
        ob_ref[rows, c0:c0 + LANES] = oa.astype(BF16)
        ob_ref[rows, c0 + LANES:c0 + 2 * LANES] = oc.astype(BF16)

    def attn_out_rows(r, h1_new_ref):
        rows = block_rows(r)
        mix = _dot(ob_ref[rows, :], wao_ref[...]) + bao_ref[...]
        first_norm(r, x_ref[0, rows, :], mix, h1_new_ref)

    def gate_up(c):
        hb = hb_ref[...]
        return (_dot(hb, wi_ref[:, c * FF_CHUNK:(c + 1) * FF_CHUNK]),
                _dot(hb, wi_ref[:, D_FF + c * FF_CHUNK:D_FF + (c + 1) * FF_CHUNK]))

    def conv_chunk(c, g, u):
        cols = slice(c * FF_CHUNK, (c + 1) * FF_CHUNK)
        if is_meta:
            g = jnp.where(jnp.concatenate([live_rows(r) for r in range(n_rb)], axis=0), g, 0.0)
        gext = jnp.concatenate([gcar_ref[:, cols], g], axis=0)
        g1 = pltpu.roll(gext, 1, 0)[SUBLANES:]
        g2 = pltpu.roll(gext, 2, 0)[SUBLANES:]
        gcar_ref[:, cols] = g[T - SUBLANES:]
        cc = (cb_ref[:, cols] + g2 * cw_ref[0:1, cols] + g1 * cw_ref[1:2, cols]
              + g * cw_ref[2:3, cols])
        hh_ref[:, cols] = (_silu(cc) * u).astype(BF16)

    def body(h1_new_ref, h1_old_ref):
        f_blocks = []
        if kind == "pool":
            for r in range(max(n_rb, N_OUT_BLOCKS)):
                if r < N_OUT_BLOCKS:
                    f_blocks.append(down_block(r))
                if r < n_rb:
                    pool_rows(r, h1_new_ref)
            tail = x_ref[0, T - MAX_POOL:, :]
            if is_meta:
                tail = jnp.where(live_rows(n_rb - 1)[BLOCK - MAX_POOL:], tail, 0.0)
            halo_ref[...] = tail
        else:
            for r in range(n_rb):
                q_rows(r)
            units = [(j, h) for j in range(n_rb) for h in range(N_KV_HEADS)]
            nxt = scores(*units[0])
            for p, (j, h) in enumerate(units):
                sc, vv = nxt
                if p + 1 < len(units):
                    nxt = scores(*units[p + 1])
                if p % N_KV_HEADS == 1 and len(f_blocks) < N_OUT_BLOCKS:
                    f_blocks.append(down_block(len(f_blocks)))
                weighted_values(j, h, softmax(j, h, sc), vv)
                if h == N_KV_HEADS - 1:
                    attn_out_rows(j, h1_new_ref)
            while len(f_blocks) < N_OUT_BLOCKS:
                f_blocks.append(down_block(len(f_blocks)))

        under = [functools.partial(finish_rows, r, f_blocks, h1_old_ref) for r in range(n_rb)]
        if emit_kv:
            under += [functools.partial(kv_block, n) for n in range(2 * KV_DUP // OUT_BLOCK)]
        nxt = gate_up(0)
        for _ in range((n_rb + 1) // 2):
            under.pop(0)()
        for c in range(N_FF_CHUNKS):
            g, u = nxt
            if c + 1 < N_FF_CHUNKS:
                nxt = gate_up(c + 1)
            conv_chunk(c, g, u)
            if under:
                under.pop(0)()
        for job in under:
            job()
        gtail_ref[0] = gcar_ref[...]

    @pl.when(jnp.logical_and(has_cur, s % 2 == 0))
    def _():
        body(h1_even_ref, h1_odd_ref)

    @pl.when(jnp.logical_and(has_cur, s % 2 == 1))
    def _():
        body(h1_odd_ref, h1_even_ref)

    @pl.when(s == n_steps - 1)
    def _():
        h1_old_ref = h1_even_ref if (n_steps - 2) % 2 == 0 else h1_odd_ref
        f_blocks = [down_block(n) for n in range(N_OUT_BLOCKS)]
        for r in range(n_rb):
            finish_rows(r, f_blocks, h1_old_ref)
        if emit_kv:
            for n in range(2 * KV_DUP // OUT_BLOCK):
                kv_block(n)

    if emit_kv:
        @pl.when(jnp.logical_and(s > 0, (s - 1) % n_tiles == n_tiles - 1))
        def _():
            kvtail_ref[0] = _dot(h2b_ref[T - WINDOW:, :], wkv_ref[...]) + bkv_ref[...]


def _prompt_layer(x, *, layer, kind, is_meta, emit_kv, tile, name, halo_in=None, gcar_in,
                  pool=None, attn=None, ln, ffn, kv=None):
    B, L, _ = x.shape
    n_tiles = L // tile
    n_all = B * n_tiles
    n_steps = n_all + 1
    nb = tile // BLOCK
    attn_idx = layer - N_A

    def cur(s):
        t = jnp.minimum(s, n_all - 1)
        return t // n_tiles, t % n_tiles

    def prev(s):
        t = jnp.maximum(s - 1, 0)
        return t // n_tiles, t % n_tiles

    cur_rows = lambda w: pl.BlockSpec((1, tile, w), lambda s: (*cur(s), 0))
    prev_rows = lambda w: pl.BlockSpec((1, tile, w), lambda s: (*prev(s), 0))
    ins, specs = [x], [cur_rows(D_MODEL)]

    def add(a, spec):
        ins.append(a)
        specs.append(spec)

    if kind == "pool":
        add(halo_in, _const_spec(halo_in))
    add(gcar_in, _const_spec(gcar_in))
    if kind == "pool":
        for a in pool:
            add(a, _layer_spec(a, layer))
    else:
        kd, vd, kmeta, vmeta, mask, sinks, wq, bq, wao, bao = attn

        def before(s):
            b, i = cur(s)
            return b, jnp.maximum(i * nb - 1, 0), 0

        before_spec = pl.BlockSpec((1, BLOCK, KV_DUP), before)
        ins += [kd, vd, kd, vd]
        specs += [cur_rows(KV_DUP), cur_rows(KV_DUP), before_spec, before_spec]
        for a in (kmeta, vmeta, mask):
            add(a, _const_spec(a))
        add(sinks, pl.BlockSpec(memory_space=pltpu.SMEM))
        for a in (wq, bq, wao, bao):
            add(a, _layer_spec(a, attn_idx))
    for a in (*ln, *ffn):
        add(a, _layer_spec(a, layer))
    if emit_kv:
        for a in kv:
            add(a, _const_spec(a))

    out_shape = [jax.ShapeDtypeStruct((B, L, D_MODEL), F32),
                 jax.ShapeDtypeStruct((B, SUBLANES, D_FF), F32)]
    out_specs = [prev_rows(D_MODEL), pl.BlockSpec((1, SUBLANES, D_FF), lambda s: (cur(s)[0], 0, 0))]
    if emit_kv:
        out_shape += [jax.ShapeDtypeStruct((B, L, KV_DUP), BF16),
                      jax.ShapeDtypeStruct((B, L, KV_DUP), BF16),
                      jax.ShapeDtypeStruct((B, WINDOW, 2 * KV_DIM), F32)]
        out_specs += [prev_rows(KV_DUP), prev_rows(KV_DUP),
                      pl.BlockSpec((1, WINDOW, 2 * KV_DIM), lambda s: (prev(s)[0], 0, 0))]
    scratch = [pltpu.VMEM((SUBLANES, D_FF), F32), pltpu.VMEM((tile, D_FF), BF16),
               pltpu.VMEM((tile, D_MODEL), F32), pltpu.VMEM((tile, D_MODEL), F32),
               pltpu.VMEM((tile, D_MODEL), BF16)]
    if kind == "pool":
        scratch.append(pltpu.VMEM((MAX_POOL, D_MODEL), F32))
    else:
        scratch += [pltpu.VMEM((tile, D_MODEL), BF16), pltpu.VMEM((tile, D_MODEL), BF16)]
    if emit_kv:
        scratch.append(pltpu.VMEM((tile, D_MODEL), BF16))
    body = functools.partial(_prompt_kernel, kind=kind, is_meta=is_meta, emit_kv=emit_kv,
                             tile=tile, n_tiles=n_tiles, n_steps=n_steps, attn_idx=attn_idx)
    return pl.pallas_call(
        body,
        grid=(n_steps,),
        in_specs=specs,
        out_specs=out_specs,
        out_shape=out_shape,
        scratch_shapes=scratch,
        compiler_params=pltpu.CompilerParams(
            dimension_semantics=("arbitrary",), vmem_limit_bytes=VMEM_LIMIT),
        name=name,
    )(*ins)


def _sample_kernel(*refs, kind, emit_kv, emit_q):
    it = iter(refs)
    x_ref = next(it)
    if kind == "pool":
        sp_ref, pw_ref, ps_ref = next(it), next(it), next(it)
    else:
        o_ref, wao_ref, bao_ref = next(it), next(it), next(it)
    lmg_ref, lmb_ref, lfg_ref, lfb_ref = (next(it) for _ in range(4))
    wg_ref, wu_ref, cw_ref, cb_ref, wo_ref, s0_ref, s1_ref = (next(it) for _ in range(7))
    if emit_kv:
        wkv_ref, bkv_ref = next(it), next(it)
    if emit_q:
        wq_ref, bq_ref = next(it), next(it)
    y_ref, g_ref = next(it), next(it)
    if emit_kv:
        kv_ref = next(it)
    if emit_q:
        q_ref = next(it)
    h1_ref, h1b_ref, acc_ref = next(it), next(it), next(it)

    c = pl.program_id(0)

    @pl.when(c == 0)
    def _():
        x = x_ref[...]
        if kind == "pool":
            parts = []
            for gi, w in enumerate(POOL_WINDOWS):
                cols = slice(gi * POOL_GC, (gi + 1) * POOL_GC)
                win = x[:, cols]
                for k in range(1, w):
                    win = win + sp_ref[POOL_STATE - k, :, cols]
                dg = win * (1.0 / w) - x[:, cols]
                parts.append(_dot(dg.astype(BF16), pw_ref[gi]))
            mix = jnp.concatenate(parts, axis=1) * ps_ref[...]
        else:
            mix = _dot(o_ref[...], wao_ref[...]) + bao_ref[...]
        h1 = _layer_norm(ALPHA * x + mix, lmg_ref[...], lmb_ref[...])
        h1_ref[...] = h1
        h1b_ref[...] = h1.astype(BF16)
        acc_ref[...] = jnp.zeros_like(acc_ref)

    h1b = h1b_ref[...]
    g = _dot(h1b, wg_ref[...])
    u = _dot(h1b, wu_ref[...])
    g_ref[...] = g
    cc = cb_ref[...] + s0_ref[...] * cw_ref[0:1, :] + s1_ref[...] * cw_ref[1:2, :] + g * cw_ref[2:3, :]
    acc_ref[...] += _dot((_silu(cc) * u).astype(BF16), wo_ref[...])

    @pl.when(c == N_FF_CHUNKS - 1)
    def _():
        h2 = _layer_norm(ALPHA * h1_ref[...] + acc_ref[...], lfg_ref[...], lfb_ref[...])
        y_ref[...] = h2
        h2b = h2.astype(BF16)
        if emit_kv:
            kv_ref[...] = _dot(h2b, wkv_ref[...]) + bkv_ref[...]
        if emit_q:
            q_ref[...] = _dot(h2b, wq_ref[...]) + bq_ref[...]


def _sample_layer(x, *, layer, kind, emit_kv, emit_q, name, mixer, ln, ffn, conv_t, kv=None, qproj=None):
    n = x.shape[0]
    ins, specs = [x], [_const_spec(x)]

    def add(a, spec):
        ins.append(a)
        specs.append(spec)

    if kind == "pool":
        for a in mixer:
            add(a, _layer_spec(a, layer))
    else:
        o, wao, bao = mixer
        add(o, _const_spec(o))
        add(wao, _layer_spec(wao, layer - N_A))
        add(bao, _layer_spec(bao, layer - N_A))
    for a in ln:
        add(a, _layer_spec(a, layer))
    wi, cw, cb, wo = ffn
    add(wi, pl.BlockSpec((None, D_MODEL, FF_CHUNK), lambda c: (layer, 0, c)))
    add(wi, pl.BlockSpec((None, D_MODEL, FF_CHUNK), lambda c: (layer, 0, N_FF_CHUNKS + c)))
    add(cw, pl.BlockSpec((None, CONV_W, FF_CHUNK), lambda c: (layer, 0, c)))
    add(cb, pl.BlockSpec((None, 1, FF_CHUNK), lambda c: (layer, 0, c)))
    add(wo, pl.BlockSpec((None, FF_CHUNK, D_MODEL), lambda c: (layer, c, 0)))
    add(conv_t, pl.BlockSpec((None, None, n, FF_CHUNK), lambda c: (layer, 0, 0, c)))
    add(conv_t, pl.BlockSpec((None, None, n, FF_CHUNK), lambda c: (layer, 1, 0, c)))
    if emit_kv:
        for a in kv:
            add(a, _const_spec(a))
    if emit_q:
        for a in qproj:
            add(a, _layer_spec(a, layer + 1 - N_A))

    full = lambda w: pl.BlockSpec((n, w), lambda c: (0, 0))
    out_shape = [jax.ShapeDtypeStruct((n, D_MODEL), F32), jax.ShapeDtypeStruct((n, D_FF), F32)]
    out_specs = [full(D_MODEL), pl.BlockSpec((n, FF_CHUNK), lambda c: (0, c))]
    if emit_kv:
        out_shape.append(jax.ShapeDtypeStruct((n, 2 * KV_DIM), F32))
        out_specs.append(full(2 * KV_DIM))
    if emit_q:
        out_shape.append(jax.ShapeDtypeStruct((n, D_MODEL), F32))
        out_specs.append(full(D_MODEL))
    body = functools.partial(_sample_kernel, kind=kind, emit_kv=emit_kv, emit_q=emit_q)
    return pl.pallas_call(
        body,
        grid=(N_FF_CHUNKS,),
        in_specs=specs,
        out_specs=out_specs,
        out_shape=out_shape,
        scratch_shapes=[pltpu.VMEM((n, D_MODEL), F32), pltpu.VMEM((n, D_MODEL), BF16),
                        pltpu.VMEM((n, D_MODEL), F32)],
        compiler_params=pltpu.CompilerParams(dimension_semantics=("arbitrary",),
                                             vmem_limit_bytes=VMEM_LIMIT),
        name=name,
    )(*ins)


def _sample_attn_kernel(*refs, update):
    it = iter(refs)
    q_ref, kt_ref, vt_ref = next(it), next(it), next(it)
    if update:
        kn_ref, vn_ref = next(it), next(it)
    sink_ref, o_ref = next(it), next(it)
    if update:
        nkt_ref, nvt_ref = next(it), next(it)

    kt, vt = kt_ref[...], vt_ref[...]
    if update:
        newest = lax.broadcasted_iota(jnp.int32, (KV_DIM, WINDOW), 1) == WINDOW - 1

        def advance(win, new_ref):
            return jnp.stack([jnp.where(newest, new_ref[:, b:b + 1], pltpu.roll(win[b], WINDOW - 1, 1))
                              for b in range(SAMPLE_GROUP)])

        kt, vt = advance(kt, kn_ref), advance(vt, vn_ref)
        nkt_ref[...] = kt
        nvt_ref[...] = vt
    q = q_ref[...]
    s = jnp.einsum("bhc,bcj->bhj", q, kt.astype(BF16), preferred_element_type=F32) * ATTN_SCALE
    sink = sink_ref[...][None]
    m = jnp.max(jnp.maximum(s, sink), axis=-1, keepdims=True)
    e = jnp.exp(s - m)
    den = jnp.sum(e, axis=-1, keepdims=True) + jnp.exp(sink - m)
    p = (e / den).astype(BF16)
    o_ref[...] = jnp.einsum("bhj,bcj->bhc", p, vt.astype(BF16), preferred_element_type=F32).astype(BF16)


def _sample_attn(qexp, kt, vt, sinkb, layer, name, new_cols=None):
    n = qexp.shape[0]
    G = SAMPLE_GROUP
    update = new_cols is not None
    win_spec = pl.BlockSpec((G, KV_DIM, WINDOW), lambda i: (i, 0, 0))
    q_spec = pl.BlockSpec((G, N_HEADS, KV_DIM), lambda i: (i, 0, 0))
    ins, specs = [qexp, kt, vt], [q_spec, win_spec, win_spec]
    if update:
        col_spec = pl.BlockSpec((None, KV_DIM, G), lambda i: (i, 0, 0))
        ins += list(new_cols)
        specs += [col_spec, col_spec]
    ins.append(sinkb)
    specs.append(pl.BlockSpec((None, N_HEADS, WINDOW), lambda i: (layer - N_A, 0, 0)))
    out_shape = [jax.ShapeDtypeStruct((n, N_HEADS, KV_DIM), BF16)]
    out_specs = [q_spec]
    if update:
        out_shape += [jax.ShapeDtypeStruct(kt.shape, F32)] * 2
        out_specs += [win_spec, win_spec]
    return pl.pallas_call(
        functools.partial(_sample_attn_kernel, update=update),
        grid=(n // G,),
        in_specs=specs,
        out_specs=out_specs,
        out_shape=out_shape,
        compiler_params=pltpu.CompilerParams(dimension_semantics=("arbitrary",)),
        name=name,
    )(*ins)


def _dup_heads(w):
    lead = w.shape[:-1]
    w = w.reshape(*lead, N_KV_HEADS, 1, HEAD_DIM)
    return jnp.broadcast_to(w, (*lead, N_KV_HEADS, 2, HEAD_DIM)).reshape(*lead, KV_DUP)


def _expand_q(q):
    n = q.shape[0]
    q = q.reshape(n, N_KV_HEADS, GROUP, HEAD_DIM)
    parts = [jnp.pad(q[:, h], ((0, 0), (0, 0), (h * HEAD_DIM, KV_DIM - (h + 1) * HEAD_DIM)))
             for h in range(N_KV_HEADS)]
    return jnp.stack(parts, axis=1).reshape(n, N_HEADS, KV_DIM)


def _contract_o(res):
    n = res.shape[0]
    res = res.reshape(n, N_KV_HEADS, GROUP, N_KV_HEADS, HEAD_DIM)
    return jnp.stack([res[:, h, :, h, :] for h in range(N_KV_HEADS)], axis=1).reshape(n, N_HEADS * HEAD_DIM)


def _band_masks():
    i = jnp.arange(BLOCK)[:, None]
    j = jnp.arange(2 * BLOCK)[None, :]
    band = (j > i) & (j <= i + BLOCK)
    masks = jnp.stack([band, band & (j >= PAD_ROWS), band & (j >= BLOCK + PAD_ROWS)])
    return jnp.where(masks, 0.0, NEG).astype(F32)


def kernel(x_prompt, x_sample, state_pool, state_conv, state_k_win, state_v_win, meta_tokens, pool_w, pool_scale, w_kv, b_kv, attn_w_q, attn_b_q, attn_sinks, attn_w_o, attn_b_o, ffn_w_in, ffn_conv_w, ffn_conv_b, ffn_w_out, ln_mix_g, ln_mix_b, ln_ffn_g, ln_ffn_b):
    B, S, _ = x_prompt.shape
    n_dec = x_sample.shape[0]
    assert x_sample.shape[1] == 1 and S % ROW_TILE == 0 and n_dec % SAMPLE_GROUP == 0

    row = lambda a: a[:, None, :]
    ffn = (ffn_w_in.astype(BF16), ffn_conv_w, row(ffn_conv_b), ffn_w_out.astype(BF16))
    ln = (row(ln_mix_g), row(ln_mix_b), row(ln_ffn_g), row(ln_ffn_b))
    pool = (pool_w.astype(BF16), row(pool_scale))
    wkv_b, bkv = w_kv.astype(BF16), b_kv[None, :]
    wkvd_b = jnp.concatenate([_dup_heads(w_kv[:, :KV_DIM]), _dup_heads(w_kv[:, KV_DIM:])], axis=1).astype(BF16)
    bkvd = jnp.concatenate([_dup_heads(b_kv[:KV_DIM]), _dup_heads(b_kv[KV_DIM:])])[None, :]
    kvw = (wkvd_b, bkvd, wkv_b, bkv)
    wq_b, bq = attn_w_q.astype(BF16), row(attn_b_q)
    wao_b, bao = attn_w_o.astype(BF16), row(attn_b_o)
    sinkb = jnp.broadcast_to(attn_sinks[:, :, None], (attn_sinks.shape[0], N_HEADS, WINDOW))
    masks = _band_masks()

    zeros_halo = jnp.zeros((MAX_POOL, D_MODEL), F32)
    zeros_gcar = jnp.zeros((SUBLANES, D_FF), F32)
    zeros_kv = jnp.zeros((BLOCK, KV_DUP), BF16)

    hm = jnp.concatenate([jnp.zeros((PAD_ROWS, D_MODEL), F32), meta_tokens.astype(F32)])[None]
    hp = x_prompt
    hs = x_sample.reshape(n_dec, D_MODEL)
    pool_t = jnp.transpose(state_pool, (0, 2, 1, 3))
    conv_t = jnp.transpose(state_conv, (0, 2, 1, 3))
    kt = jnp.transpose(state_k_win, (0, 2, 3, 1)).reshape(n_dec, KV_DIM, WINDOW)
    vt = jnp.transpose(state_v_win, (0, 2, 3, 1)).reshape(n_dec, KV_DIM, WINDOW)

    new_pool_p, new_pool_t, new_conv_p, new_conv_t = [], [], [], []
    for l in range(DEPTH):
        if l < N_A:
            emit_kv = l == N_A - 1
            new_pool_p.append(hp[:, S - POOL_STATE:])
            new_pool_t.append(jnp.concatenate([pool_t[l, 1:], hs[None]], axis=0))
            outs_m = _prompt_layer(hm, layer=l, kind="pool", is_meta=True, emit_kv=emit_kv, tile=BLOCK,
                                   name=f"meta_l{l}", halo_in=zeros_halo, gcar_in=zeros_gcar,
                                   pool=pool, ln=ln, ffn=ffn, kv=kvw)
            outs_p = _prompt_layer(hp, layer=l, kind="pool", is_meta=False, emit_kv=emit_kv, tile=ROW_TILE,
                                   name=f"prompt_l{l}", halo_in=hm[0, PAD_ROWS:], gcar_in=outs_m[1][0],
                                   pool=pool, ln=ln, ffn=ffn, kv=kvw)
            outs_s = _sample_layer(hs, layer=l, kind="pool", emit_kv=emit_kv, emit_q=emit_kv,
                                   name=f"sample_l{l}", mixer=(pool_t, *pool), ln=ln, ffn=ffn, conv_t=conv_t,
                                   kv=(wkv_b, bkv), qproj=(wq_b, bq))
            if emit_kv:
                kd_m, vd_m = outs_m[2][0], outs_m[3][0]
                kd_p, vd_p, kvtail = outs_p[2], outs_p[3], outs_p[4]
                kv_s, q_s = outs_s[2], outs_s[3]
        else:
            attn_w = (masks, attn_sinks, wq_b, bq, wao_b, bao)
            outs_m = _prompt_layer(hm, layer=l, kind="attn", is_meta=True, emit_kv=False, tile=BLOCK,
                                   name=f"meta_l{l}", gcar_in=zeros_gcar,
                                   attn=(kd_m[None], vd_m[None], zeros_kv, zeros_kv, *attn_w), ln=ln, ffn=ffn)
            outs_p = _prompt_layer(hp, layer=l, kind="attn", is_meta=False, emit_kv=False, tile=ROW_TILE,
                                   name=f"prompt_l{l}", gcar_in=outs_m[1][0],
                                   attn=(kd_p, vd_p, kd_m, vd_m, *attn_w), ln=ln, ffn=ffn)
            qexp = _expand_q(q_s.astype(BF16))
            if l == N_A:
                cols = lambda a: jnp.transpose(a.reshape(n_dec // SAMPLE_GROUP, SAMPLE_GROUP, KV_DIM), (0, 2, 1))
                res, kt, vt = _sample_attn(qexp, kt, vt, sinkb, l, f"sample_attn_l{l}",
                                           new_cols=(cols(kv_s[:, :KV_DIM]), cols(kv_s[:, KV_DIM:])))
            else:
                res = _sample_attn(qexp, kt, vt, sinkb, l, f"sample_attn_l{l}")[0]
            emit_q = l + 1 < DEPTH
            outs_s = _sample_layer(hs, layer=l, kind="attn", emit_kv=False, emit_q=emit_q, name=f"sample_l{l}",
                                   mixer=(_contract_o(res), wao_b, bao), ln=ln, ffn=ffn, conv_t=conv_t,
                                   qproj=(wq_b, bq))
            if emit_q:
                q_s = outs_s[2]
        new_conv_p.append(outs_p[1][:, SUBLANES - (CONV_W - 1):])
        new_conv_t.append(jnp.stack([conv_t[l, 1], outs_s[1]]))
        hm, hp, hs = outs_m[0], outs_p[0], outs_s[0]

    unwin = lambda t: jnp.transpose(t.reshape(-1, N_KV_HEADS, HEAD_DIM, WINDOW), (0, 3, 1, 2))
    new_k_p = kvtail[:, :, :KV_DIM].reshape(B, WINDOW, N_KV_HEADS, HEAD_DIM)
    new_v_p = kvtail[:, :, KV_DIM:].reshape(B, WINDOW, N_KV_HEADS, HEAD_DIM)
    return (hp, hs.reshape(n_dec, 1, D_MODEL),
            jnp.stack(new_pool_p), jnp.transpose(jnp.stack(new_pool_t), (0, 2, 1, 3)),
            jnp.stack(new_conv_p), jnp.transpose(jnp.stack(new_conv_t), (0, 2, 1, 3)),
            new_k_p, new_v_p, unwin(kt), unwin(vt))
```

```python
import functools

import jax
import jax.numpy as jnp
from jax import lax
from jax.experimental import pallas as pl
from jax.experimental.pallas import tpu as pltpu

F32 = jnp.float32
BF16 = jnp.bfloat16

D_MODEL = 1024
DEPTH = 4
N_META = 16
N_A = DEPTH // 2
POOL_WINDOWS = (2, 4, 8, 16)
N_POOL_GROUPS = len(POOL_WINDOWS)
POOL_GC = D_MODEL // N_POOL_GROUPS
MAX_POOL = max(POOL_WINDOWS)
POOL_STATE = MAX_POOL - 1
HEAD_DIM = 64
N_HEADS = D_MODEL // HEAD_DIM
N_KV_HEADS = 4
GROUP = N_HEADS // N_KV_HEADS
KV_DIM = N_KV_HEADS * HEAD_DIM
WINDOW = 128
BLOCK = 128
ATTN_SCALE = HEAD_DIM ** -0.5
D_FF = 2816
CONV_W = 3
ALPHA = (2.0 * DEPTH) ** 0.25
LN_EPS = 1e-5
NEG = -1e30

SUBLANES = 8
LANES = 128
PAD_ROWS = BLOCK - N_META
FF_CHUNK = 256
N_FF_CHUNKS = D_FF // FF_CHUNK
OUT_BLOCK = 256
N_OUT_BLOCKS = D_MODEL // OUT_BLOCK
DOWN_ROWS = 128
GATE_ROWS = 128
ROW_TILE = 512
SAMPLE_GROUP = 16
DUP = 2 * HEAD_DIM
KV_DUP = N_KV_HEADS * DUP
VMEM_LIMIT = 58 * 1024 * 1024


def _layer_norm(z, g, b):
    mu = jnp.mean(z, axis=-1, keepdims=True)
    zc = z - mu
    var = jnp.mean(zc * zc, axis=-1, keepdims=True)
    return zc * lax.rsqrt(var + LN_EPS) * g + b


def _silu(c):
    return c * (1.0 / (1.0 + jnp.exp(-c)))


def _dot(a, b):
    return jnp.dot(a, b, preferred_element_type=F32)


def _const_spec(a):
    nd = a.ndim
    return pl.BlockSpec(a.shape, lambda *_: (0,) * nd, pipeline_mode=pl.Buffered(1))


def _layer_spec(a, l):
    nd = a.ndim
    return pl.BlockSpec((None,) + a.shape[1:], lambda *_: (l,) + (0,) * (nd - 1),
                        pipeline_mode=pl.Buffered(1))


def _prompt_kernel(*refs, kind, is_meta, emit_kv, tile, n_tiles, n_steps, attn_idx):
    it = iter(refs)
    x_ref = next(it)
    halo_in_ref = next(it) if kind == "pool" else None
    gcar_in_ref = next(it)
    if kind == "pool":
        pw_ref, ps_ref = next(it), next(it)
    else:
        kcur_ref, vcur_ref, kprev_ref, vprev_ref, kmeta_ref, vmeta_ref = (next(it) for _ in range(6))
        mask_ref, sinks_ref = next(it), next(it)
        wq_ref, bq_ref, wao_ref, bao_ref = (next(it) for _ in range(4))
    lmg_ref, lmb_ref, lfg_ref, lfb_ref = (next(it) for _ in range(4))
    wi_ref, cw_ref, cb_ref, wo_ref = (next(it) for _ in range(4))
    if emit_kv:
        wkvd_ref, bkvd_ref, wkv_ref, bkv_ref = (next(it) for _ in range(4))
    y_ref, gtail_ref = next(it), next(it)
    if emit_kv:
        kd_ref, vd_ref, kvtail_ref = next(it), next(it), next(it)
    gcar_ref, hh_ref, h1_even_ref, h1_odd_ref, hb_ref = (next(it) for _ in range(5))
    if kind == "pool":
        halo_ref = next(it)
    else:
        qb_ref, ob_ref = next(it), next(it)
    if emit_kv:
        h2b_ref = next(it)

    T = tile
    n_rb = T // BLOCK
    s = pl.program_id(0)
    i = s % n_tiles
    has_cur = s < n_steps - 1

    @pl.when(s == 0)
    def _():
        hh_ref[...] = jnp.zeros_like(hh_ref)
        h1_even_ref[...] = jnp.zeros_like(h1_even_ref)
        h1_odd_ref[...] = jnp.zeros_like(h1_odd_ref)

    @pl.when(jnp.logical_and(has_cur, i == 0))
    def _():
        gcar_ref[...] = gcar_in_ref[...]
        if kind == "pool":
            halo_ref[...] = halo_in_ref[...]

    def block_rows(r):
        return slice(r * BLOCK, (r + 1) * BLOCK)

    def live_rows(r):
        return lax.broadcasted_iota(jnp.int32, (BLOCK, 1), 0) + r * BLOCK >= PAD_ROWS

    down_rows = min(DOWN_ROWS, T)
    down_ids = [(n, m) for m in range(T // down_rows) for n in range(N_OUT_BLOCKS)]

    def down_piece(f_pieces, n, m):
        f_pieces[n, m] = _dot(hh_ref[m * down_rows:(m + 1) * down_rows, :],
                              wo_ref[:, n * OUT_BLOCK:(n + 1) * OUT_BLOCK])

    def finish_rows(r, f_pieces, h1_old_ref):
        rows = block_rows(r)
        m, off = divmod(r * BLOCK, down_rows)
        f = jnp.concatenate([f_pieces[n, m][off:off + BLOCK] for n in range(N_OUT_BLOCKS)], axis=1)
        h2 = _layer_norm(ALPHA * h1_old_ref[rows, :] + f, lfg_ref[...], lfb_ref[...])
        y_ref[0, rows, :] = h2
        if emit_kv:
            h2b_ref[rows, :] = h2.astype(BF16)

    def kv_block(n):
        cols = slice(n * OUT_BLOCK, (n + 1) * OUT_BLOCK)
        kvd = (_dot(h2b_ref[...], wkvd_ref[:, cols]) + bkvd_ref[:, cols]).astype(BF16)
        if n * OUT_BLOCK < KV_DUP:
            kd_ref[0, :, cols] = kvd
        else:
            vd_ref[0, :, n * OUT_BLOCK - KV_DUP:(n + 1) * OUT_BLOCK - KV_DUP] = kvd

    def first_norm(r, x_blk, mix, h1_new_ref):
        rows = block_rows(r)
        h1 = _layer_norm(ALPHA * x_blk + mix, lmg_ref[...], lmb_ref[...])
        h1_new_ref[rows, :] = h1
        hb_ref[rows, :] = h1.astype(BF16)

    def pool_rows(r, h1_new_ref):
        rows = block_rows(r)
        x_blk = x_ref[0, rows, :]
        if is_meta:
            x_blk = jnp.where(live_rows(r), x_blk, 0.0)
        if r == 0:
            xe = jnp.concatenate([halo_ref[...], x_blk], axis=0)
        else:
            xe = x_ref[0, r * BLOCK - MAX_POOL:(r + 1) * BLOCK, :]
        s2 = xe + pltpu.roll(xe, 1, 0)
        s2r = s2[:, POOL_GC:]
        s4 = s2r + pltpu.roll(s2r, 2, 0)
        s4r = s4[:, POOL_GC:]
        s8 = s4r + pltpu.roll(s4r, 4, 0)
        s8r = s8[:, POOL_GC:]
        s16 = s8r + pltpu.roll(s8r, 8, 0)
        wins = (s2[MAX_POOL:, :POOL_GC], s4[MAX_POOL:, :POOL_GC],
                s8[MAX_POOL:, :POOL_GC], s16[MAX_POOL:])
        parts = []
        for gi, w in enumerate(POOL_WINDOWS):
            xg = x_blk[:, gi * POOL_GC:(gi + 1) * POOL_GC]
            if is_meta:
                t1 = lax.broadcasted_iota(jnp.int32, (BLOCK, 1), 0) + (r * BLOCK - PAD_ROWS + 1)
                dg = wins[gi] / jnp.clip(t1, 1, w).astype(F32) - xg
            else:
                dg = wins[gi] * (1.0 / w) - xg
            parts.append(_dot(dg.astype(BF16), pw_ref[gi]))
        mix = jnp.concatenate(parts, axis=1) * ps_ref[...]
        first_norm(r, x_blk, mix, h1_new_ref)

    def q_rows(r):
        rows = block_rows(r)
        qb_ref[rows, :] = (_dot(x_ref[0, rows, :].astype(BF16), wq_ref[...]) + bq_ref[...]).astype(BF16)

    lo = lax.broadcasted_iota(jnp.int32, (BLOCK, LANES), 1) < HEAD_DIM

    def keys_values(j, h):
        cols = slice(h * DUP, (h + 1) * DUP)
        if j == 0:
            first = i == 0
            kp = jnp.where(first, kmeta_ref[:, cols], kprev_ref[0, :, cols])
            vp = jnp.where(first, vmeta_ref[:, cols], vprev_ref[0, :, cols])
            return (jnp.concatenate([kp, kcur_ref[0, 0:BLOCK, cols]], axis=0),
                    jnp.concatenate([vp, vcur_ref[0, 0:BLOCK, cols]], axis=0))
        return (kcur_ref[0, (j - 1) * BLOCK:(j + 1) * BLOCK, cols],
                vcur_ref[0, (j - 1) * BLOCK:(j + 1) * BLOCK, cols])

    def scores(j, h):
        rows = block_rows(j)
        c0 = h * GROUP * HEAD_DIM
        qa = qb_ref[rows, c0:c0 + LANES]
        qc = qb_ref[rows, c0 + LANES:c0 + 2 * LANES]
        z = jnp.zeros_like(qa)
        lhs = jnp.concatenate([jnp.where(lo, qa, z), jnp.where(lo, z, qa),
                               jnp.where(lo, qc, z), jnp.where(lo, z, qc)], axis=0)
        kk, vv = keys_values(j, h)
        sc = lax.dot_general(lhs, kk, (((1,), (1,)), ((), ())), preferred_element_type=F32)
        return sc * ATTN_SCALE, vv

    def softmax(j, h, sc):
        if is_meta:
            madd = mask_ref[2]
        elif j == 0:
            madd = mask_ref[jnp.where(i == 0, 1, 0)]
        else:
            madd = mask_ref[0]
        ps = []
        for g in range(GROUP):
            sg = sc[g * BLOCK:(g + 1) * BLOCK] + madd
            sink = sinks_ref[attn_idx, h * GROUP + g]
            m = jnp.maximum(jnp.max(sg, axis=-1, keepdims=True), sink)
            e = jnp.exp(sg - m)
            den = jnp.sum(e, axis=-1, keepdims=True) + jnp.exp(sink - m)
            ps.append((e / den).astype(BF16))
        return jnp.concatenate(ps, axis=0)

    def weighted_values(j, h, p, vv):
        rows = block_rows(j)
        c0 = h * GROUP * HEAD_DIM
        res = _dot(p, vv)
        oa = jnp.where(lo, res[0:BLOCK], res[BLOCK:2 * BLOCK])
        oc = jnp.where(lo, res[2 * BLOCK:3 * BLOCK], res[3 * BLOCK:])
        ob_ref[rows, c0:c0 + LANES] = oa.astype(BF16)
        ob_ref[rows, c0 + LANES:c0 + 2 * LANES] = oc.astype(BF16)

    def attn_out_rows(r, h1_new_ref):
        rows = block_rows(r)
        mix = _dot(ob_ref[rows, :], wao_ref[...]) + bao_ref[...]
        first_norm(r, x_ref[0, rows, :], mix, h1_new_ref)

    gu_rows = min(GATE_ROWS, T)
    gu_ids = [(c, m) for c in range(N_FF_CHUNKS) for m in range(T // gu_rows)]

    def gate_up(c, m):
        hb = hb_ref[m * gu_rows:(m + 1) * gu_rows, :]
        return (_dot(hb, wi_ref[:, 2 * c * FF_CHUNK:(2 * c + 1) * FF_CHUNK]),
                _dot(hb, wi_ref[:, (2 * c + 1) * FF_CHUNK:(2 * c + 2) * FF_CHUNK]))

    def conv_rows(c, m, g, u):
        cols = slice(c * FF_CHUNK, (c + 1) * FF_CHUNK)
        if is_meta:
            g = jnp.where(jnp.concatenate([live_rows(r) for r in range(n_rb)], axis=0), g, 0.0)
        gext = jnp.concatenate([gcar_ref[:, cols], g], axis=0)
        g1 = pltpu.roll(gext, 1, 0)[SUBLANES:]
        g2 = pltpu.roll(gext, 2, 0)[SUBLANES:]
        gcar_ref[:, cols] = g[gu_rows - SUBLANES:]
        cc = (cb_ref[:, cols] + g2 * cw_ref[0:1, cols] + g1 * cw_ref[1:2, cols]
              + g * cw_ref[2:3, cols])
        hh_ref[m * gu_rows:(m + 1) * gu_rows, cols] = (_silu(cc) * u).astype(BF16)

    def body(h1_new_ref, h1_old_ref):
        f_pieces = {}
        downs = [functools.partial(down_piece, f_pieces, n, m) for n, m in down_ids]
        if kind == "pool":
            per = -(-len(downs) // n_rb)
            for r in range(n_rb):
                for _ in range(min(per, len(downs))):
                    downs.pop(0)()
                pool_rows(r, h1_new_ref)
            tail = x_ref[0, T - MAX_POOL:, :]
            if is_meta:
                tail = jnp.where(live_rows(n_rb - 1)[BLOCK - MAX_POOL:], tail, 0.0)
            halo_ref[...] = tail
        else:
            for r in range(n_rb):
                q_rows(r)
            units = [(j, h) for j in range(n_rb) for h in range(N_KV_HEADS)]
            per = -(-len(downs) // len(units))
            nxt = scores(*units[0])
            for p, (j, h) in enumerate(units):
                sc, vv = nxt
                if p + 1 < len(units):
                    nxt = scores(*units[p + 1])
                for _ in range(min(per, len(downs))):
                    downs.pop(0)()
                weighted_values(j, h, softmax(j, h, sc), vv)
                if h == N_KV_HEADS - 1:
                    attn_out_rows(j, h1_new_ref)
        for job in downs:
            job()

        under = [functools.partial(finish_rows, r, f_pieces, h1_old_ref) for r in range(n_rb)]
        if emit_kv:
            under += [functools.partial(kv_block, n) for n in range(2 * KV_DUP // OUT_BLOCK)]
        nxt = gate_up(*gu_ids[0])
        for k, (c, m) in enumerate(gu_ids):
            g, u = nxt
            if k + 1 < len(gu_ids):
                nxt = gate_up(*gu_ids[k + 1])
            conv_rows(c, m, g, u)
            if under and (k + 1) % (T // gu_rows) == 0:
                under.pop(0)()
        for job in under:
            job()
        gtail_ref[0] = gcar_ref[...]

    @pl.when(jnp.logical_and(has_cur, s % 2 == 0))
    def _():
        body(h1_even_ref, h1_odd_ref)

    @pl.when(jnp.logical_and(has_cur, s % 2 == 1))
    def _():
        body(h1_odd_ref, h1_even_ref)

    @pl.when(s == n_steps - 1)
    def _():
        h1_old_ref = h1_even_ref if (n_steps - 2) % 2 == 0 else h1_odd_ref
        f_pieces = {}
        for n, m in down_ids:
            down_piece(f_pieces, n, m)
        for r in range(n_rb):
            finish_rows(r, f_pieces, h1_old_ref)
        if emit_kv:
            for n in range(2 * KV_DUP // OUT_BLOCK):
                kv_block(n)

    if emit_kv:
        @pl.when(jnp.logical_and(s > 0, (s - 1) % n_tiles == n_tiles - 1))
        def _():
            kvtail_ref[0] = _dot(h2b_ref[T - WINDOW:, :], wkv_ref[...]) + bkv_ref[...]


def _prompt_layer(x, *, layer, kind, is_meta, emit_kv, tile, name, halo_in=None, gcar_in,
                  pool=None, attn=None, ln, ffn, kv=None):
    B, L, _ = x.shape
    n_tiles = L // tile
    n_all = B * n_tiles
    n_steps = n_all + 1
    nb = tile // BLOCK
    attn_idx = layer - N_A

    def cur(s):
        t = jnp.minimum(s, n_all - 1)
        return t // n_tiles, t % n_tiles

    def prev(s):
        t = jnp.maximum(s - 1, 0)
        return t // n_tiles, t % n_tiles

    cur_rows = lambda w: pl.BlockSpec((1, tile, w), lambda s: (*cur(s), 0))
    prev_rows = lambda w: pl.BlockSpec((1, tile, w), lambda s: (*prev(s), 0))
    ins, specs = [x], [cur_rows(D_MODEL)]

    def add(a, spec):
        ins.append(a)
        specs.append(spec)

    if kind == "pool":
        add(halo_in, _const_spec(halo_in))
    add(gcar_in, _const_spec(gcar_in))
    if kind == "pool":
        for a in pool:
            add(a, _layer_spec(a, layer))
    else:
        kd, vd, kmeta, vmeta, mask, sinks, wq, bq, wao, bao = attn

        def before(s):
            b, i = cur(s)
            return b, jnp.maximum(i * nb - 1, 0), 0

        before_spec = pl.BlockSpec((1, BLOCK, KV_DUP), before)
        ins += [kd, vd, kd, vd]
        specs += [cur_rows(KV_DUP), cur_rows(KV_DUP), before_spec, before_spec]
        for a in (kmeta, vmeta, mask):
            add(a, _const_spec(a))
        add(sinks, pl.BlockSpec(memory_space=pltpu.SMEM))
        for a in (wq, bq, wao, bao):
            add(a, _layer_spec(a, attn_idx))
    for a in (*ln, *ffn):
        add(a, _layer_spec(a, layer))
    if emit_kv:
        for a in kv:
            add(a, _const_spec(a))

    out_shape = [jax.ShapeDtypeStruct((B, L, D_MODEL), F32),
                 jax.ShapeDtypeStruct((B, SUBLANES, D_FF), F32)]
    out_specs = [prev_rows(D_MODEL), pl.BlockSpec((1, SUBLANES, D_FF), lambda s: (cur(s)[0], 0, 0))]
    if emit_kv:
        out_shape += [jax.ShapeDtypeStruct((B, L, KV_DUP), BF16),
                      jax.ShapeDtypeStruct((B, L, KV_DUP), BF16),
                      jax.ShapeDtypeStruct((B, WINDOW, 2 * KV_DIM), F32)]
        out_specs += [prev_rows(KV_DUP), prev_rows(KV_DUP),
                      pl.BlockSpec((1, WINDOW, 2 * KV_DIM), lambda s: (prev(s)[0], 0, 0))]
    scratch = [pltpu.VMEM((SUBLANES, D_FF), F32), pltpu.VMEM((tile, D_FF), BF16),
               pltpu.VMEM((tile, D_MODEL), F32), pltpu.VMEM((tile, D_MODEL), F32),
               pltpu.VMEM((tile, D_MODEL), BF16)]
    if kind == "pool":
        scratch.append(pltpu.VMEM((MAX_POOL, D_MODEL), F32))
    else:
        scratch += [pltpu.VMEM((tile, D_MODEL), BF16), pltpu.VMEM((tile, D_MODEL), BF16)]
    if emit_kv:
        scratch.append(pltpu.VMEM((tile, D_MODEL), BF16))
    body = functools.partial(_prompt_kernel, kind=kind, is_meta=is_meta, emit_kv=emit_kv,
                             tile=tile, n_tiles=n_tiles, n_steps=n_steps, attn_idx=attn_idx)
    return pl.pallas_call(
        body,
        grid=(n_steps,),
        in_specs=specs,
        out_specs=out_specs,
        out_shape=out_shape,
        scratch_shapes=scratch,
        compiler_params=pltpu.CompilerParams(
            dimension_semantics=("arbitrary",), vmem_limit_bytes=VMEM_LIMIT),
        name=name,
    )(*ins)


def _sample_kernel(*refs, kind, emit_kv, emit_q):
    it = iter(refs)
    x_ref = next(it)
    if kind == "pool":
        sp_ref, pw_ref, ps_ref = next(it), next(it), next(it)
    else:
        o_ref, wao_ref, bao_ref = next(it), next(it), next(it)
    lmg_ref, lmb_ref, lfg_ref, lfb_ref = (next(it) for _ in range(4))
    wgu_ref, cw_ref, cb_ref, wo_ref, s0_ref, s1_ref = (next(it) for _ in range(6))
    if emit_kv:
        wkv_ref, bkv_ref = next(it), next(it)
    if emit_q:
        wq_ref, bq_ref = next(it), next(it)
    y_ref, g_ref = next(it), next(it)
    if emit_kv:
        kv_ref = next(it)
    if emit_q:
        q_ref = next(it)
    h1_ref, h1b_ref, acc_ref = next(it), next(it), next(it)

    c = pl.program_id(0)

    @pl.when(c == 0)
    def _():
        x = x_ref[...]
        if kind == "pool":
            parts = []
            for gi, w in enumerate(POOL_WINDOWS):
                cols = slice(gi * POOL_GC, (gi + 1) * POOL_GC)
                win = x[:, cols]
                for k in range(1, w):
                    win = win + sp_ref[POOL_STATE - k, :, cols]
                dg = win * (1.0 / w) - x[:, cols]
                parts.append(_dot(dg.astype(BF16), pw_ref[gi]))
            mix = jnp.concatenate(parts, axis=1) * ps_ref[...]
        else:
            mix = _dot(o_ref[...], wao_ref[...]) + bao_ref[...]
        h1 = _layer_norm(ALPHA * x + mix, lmg_ref[...], lmb_ref[...])
        h1_ref[...] = h1
        h1b_ref[...] = h1.astype(BF16)
        acc_ref[...] = jnp.zeros_like(acc_ref)

    h1b = h1b_ref[...]
    gu = _dot(h1b, wgu_ref[...])
    g, u = gu[:, :FF_CHUNK], gu[:, FF_CHUNK:]
    g_ref[...] = g
    cc = cb_ref[...] + s0_ref[...] * cw_ref[0:1, :] + s1_ref[...] * cw_ref[1:2, :] + g * cw_ref[2:3, :]
    acc_ref[...] += _dot((_silu(cc) * u).astype(BF16), wo_ref[...])

    @pl.when(c == N_FF_CHUNKS - 1)
    def _():
        h2 = _layer_norm(ALPHA * h1_ref[...] + acc_ref[...], lfg_ref[...], lfb_ref[...])
        y_ref[...] = h2
        h2b = h2.astype(BF16)
        if emit_kv:
            kv_ref[...] = _dot(h2b, wkv_ref[...]) + bkv_ref[...]
        if emit_q:
            q_ref[...] = _dot(h2b, wq_ref[...]) + bq_ref[...]


def _sample_layer(x, *, layer, kind, emit_kv, emit_q, name, mixer, ln, ffn, conv_t, kv=None, qproj=None):
    n = x.shape[0]
    ins, specs = [x], [_const_spec(x)]

    def add(a, spec):
        ins.append(a)
        specs.append(spec)

    if kind == "pool":
        for a in mixer:
            add(a, _layer_spec(a, layer))
    else:
        o, wao, bao = mixer
        add(o, _const_spec(o))
        add(wao, _layer_spec(wao, layer - N_A))
        add(bao, _layer_spec(bao, layer - N_A))
    for a in ln:
        add(a, _layer_spec(a, layer))
    wi, cw, cb, wo = ffn
    add(wi, pl.BlockSpec((None, D_MODEL, 2 * FF_CHUNK), lambda c: (layer, 0, c)))
    add(cw, pl.BlockSpec((None, CONV_W, FF_CHUNK), lambda c: (layer, 0, c)))
    add(cb, pl.BlockSpec((None, 1, FF_CHUNK), lambda c: (layer, 0, c)))
    add(wo, pl.BlockSpec((None, FF_CHUNK, D_MODEL), lambda c: (layer, c, 0)))
    add(conv_t, pl.BlockSpec((None, None, n, FF_CHUNK), lambda c: (layer, 0, 0, c)))
    add(conv_t, pl.BlockSpec((None, None, n, FF_CHUNK), lambda c: (layer, 1, 0, c)))
    if emit_kv:
        for a in kv:
            add(a, _const_spec(a))
    if emit_q:
        for a in qproj:
            add(a, _layer_spec(a, layer + 1 - N_A))

    full = lambda w: pl.BlockSpec((n, w), lambda c: (0, 0))
    out_shape = [jax.ShapeDtypeStruct((n, D_MODEL), F32), jax.ShapeDtypeStruct((n, D_FF), F32)]
    out_specs = [full(D_MODEL), pl.BlockSpec((n, FF_CHUNK), lambda c: (0, c))]
    if emit_kv:
        out_shape.append(jax.ShapeDtypeStruct((n, 2 * KV_DIM), F32))
        out_specs.append(full(2 * KV_DIM))
    if emit_q:
        out_shape.append(jax.ShapeDtypeStruct((n, D_MODEL), F32))
        out_specs.append(full(D_MODEL))
    body = functools.partial(_sample_kernel, kind=kind, emit_kv=emit_kv, emit_q=emit_q)
    return pl.pallas_call(
        body,
        grid=(N_FF_CHUNKS,),
        in_specs=specs,
        out_specs=out_specs,
        out_shape=out_shape,
        scratch_shapes=[pltpu.VMEM((n, D_MODEL), F32), pltpu.VMEM((n, D_MODEL), BF16),
                        pltpu.VMEM((n, D_MODEL), F32)],
        compiler_params=pltpu.CompilerParams(dimension_semantics=("arbitrary",),
                                             vmem_limit_bytes=VMEM_LIMIT),
        name=name,
    )(*ins)


def _sample_attn_kernel(*refs, update):
    it = iter(refs)
    q_ref, kt_ref, vt_ref = next(it), next(it), next(it)
    if update:
        kn_ref, vn_ref = next(it), next(it)
    sink_ref, o_ref = next(it), next(it)
    if update:
        nkt_ref, nvt_ref = next(it), next(it)

    kt, vt = kt_ref[...], vt_ref[...]
    if update:
        newest = lax.broadcasted_iota(jnp.int32, (KV_DIM, WINDOW), 1) == WINDOW - 1

        def advance(win, new_ref):
            return jnp.stack([jnp.where(newest, new_ref[:, b:b + 1], pltpu.roll(win[b], WINDOW - 1, 1))
                              for b in range(SAMPLE_GROUP)])

        kt, vt = advance(kt, kn_ref), advance(vt, vn_ref)
        nkt_ref[...] = kt
        nvt_ref[...] = vt
    q = q_ref[...]
    s = jnp.einsum("bhc,bcj->bhj", q, kt.astype(BF16), preferred_element_type=F32) * ATTN_SCALE
    sink = sink_ref[...][None]
    m = jnp.max(jnp.maximum(s, sink), axis=-1, keepdims=True)
    e = jnp.exp(s - m)
    den = jnp.sum(e, axis=-1, keepdims=True) + jnp.exp(sink - m)
    p = (e / den).astype(BF16)
    o_ref[...] = jnp.einsum("bhj,bcj->bhc", p, vt.astype(BF16), preferred_element_type=F32).astype(BF16)


def _sample_attn(qexp, kt, vt, sinkb, layer, name, new_cols=None):
    n = qexp.shape[0]
    G = SAMPLE_GROUP
    update = new_cols is not None
    win_spec = pl.BlockSpec((G, KV_DIM, WINDOW), lambda i: (i, 0, 0))
    q_spec = pl.BlockSpec((G, N_HEADS, KV_DIM), lambda i: (i, 0, 0))
    ins, specs = [qexp, kt, vt], [q_spec, win_spec, win_spec]
    if update:
        col_spec = pl.BlockSpec((None, KV_DIM, G), lambda i: (i, 0, 0))
        ins += list(new_cols)
        specs += [col_spec, col_spec]
    ins.append(sinkb)
    specs.append(pl.BlockSpec((None, N_HEADS, WINDOW), lambda i: (layer - N_A, 0, 0)))
    out_shape = [jax.ShapeDtypeStruct((n, N_HEADS, KV_DIM), BF16)]
    out_specs = [q_spec]
    if update:
        out_shape += [jax.ShapeDtypeStruct(kt.shape, F32)] * 2
        out_specs += [win_spec, win_spec]
    return pl.pallas_call(
        functools.partial(_sample_attn_kernel, update=update),
        grid=(n // G,),
        in_specs=specs,
        out_specs=out_specs,
        out_shape=out_shape,
        compiler_params=pltpu.CompilerParams(dimension_semantics=("arbitrary",)),
        name=name,
    )(*ins)


def _dup_heads(w):
    lead = w.shape[:-1]
    w = w.reshape(*lead, N_KV_HEADS, 1, HEAD_DIM)
    return jnp.broadcast_to(w, (*lead, N_KV_HEADS, 2, HEAD_DIM)).reshape(*lead, KV_DUP)


def _expand_q(q):
    n = q.shape[0]
    q = q.reshape(n, N_KV_HEADS, GROUP, HEAD_DIM)
    parts = [jnp.pad(q[:, h], ((0, 0), (0, 0), (h * HEAD_DIM, KV_DIM - (h + 1) * HEAD_DIM)))
             for h in range(N_KV_HEADS)]
    return jnp.stack(parts, axis=1).reshape(n, N_HEADS, KV_DIM)


def _contract_o(res):
    n = res.shape[0]
    res = res.reshape(n, N_KV_HEADS, GROUP, N_KV_HEADS, HEAD_DIM)
    return jnp.stack([res[:, h, :, h, :] for h in range(N_KV_HEADS)], axis=1).reshape(n, N_HEADS * HEAD_DIM)


def _band_masks():
    i = jnp.arange(BLOCK)[:, None]
    j = jnp.arange(2 * BLOCK)[None, :]
    band = (j > i) & (j <= i + BLOCK)
    masks = jnp.stack([band, band & (j >= PAD_ROWS), band & (j >= BLOCK + PAD_ROWS)])
    return jnp.where(masks, 0.0, NEG).astype(F32)


def kernel(x_prompt, x_sample, state_pool, state_conv, state_k_win, state_v_win, meta_tokens, pool_w, pool_scale, w_kv, b_kv, attn_w_q, attn_b_q, attn_sinks, attn_w_o, attn_b_o, ffn_w_in, ffn_conv_w, ffn_conv_b, ffn_w_out, ln_mix_g, ln_mix_b, ln_ffn_g, ln_ffn_b):
    B, S, _ = x_prompt.shape
    n_dec = x_sample.shape[0]
    assert x_sample.shape[1] == 1 and S % ROW_TILE == 0 and n_dec % SAMPLE_GROUP == 0

    row = lambda a: a[:, None, :]
    w_gu = jnp.transpose(ffn_w_in.reshape(DEPTH, D_MODEL, 2, N_FF_CHUNKS, FF_CHUNK), (0, 1, 3, 2, 4))
    ffn = (w_gu.reshape(DEPTH, D_MODEL, 2 * D_FF).astype(BF16), ffn_conv_w, row(ffn_conv_b),
           ffn_w_out.astype(BF16))
    ln = (row(ln_mix_g), row(ln_mix_b), row(ln_ffn_g), row(ln_ffn_b))
    pool = (pool_w.astype(BF16), row(pool_scale))
    wkv_b, bkv = w_kv.astype(BF16), b_kv[None, :]
    wkvd_b = jnp.concatenate([_dup_heads(w_kv[:, :KV_DIM]), _dup_heads(w_kv[:, KV_DIM:])], axis=1).astype(BF16)
    bkvd = jnp.concatenate([_dup_heads(b_kv[:KV_DIM]), _dup_heads(b_kv[KV_DIM:])])[None, :]
    kvw = (wkvd_b, bkvd, wkv_b, bkv)
    wq_b, bq = attn_w_q.astype(BF16), row(attn_b_q)
    wao_b, bao = attn_w_o.astype(BF16), row(attn_b_o)
    sinkb = jnp.broadcast_to(attn_sinks[:, :, None], (attn_sinks.shape[0], N_HEADS, WINDOW))
    masks = _band_masks()

    zeros_halo = jnp.zeros((MAX_POOL, D_MODEL), F32)
    zeros_gcar = jnp.zeros((SUBLANES, D_FF), F32)
    zeros_kv = jnp.zeros((BLOCK, KV_DUP), BF16)

    hm = jnp.concatenate([jnp.zeros((PAD_ROWS, D_MODEL), F32), meta_tokens.astype(F32)])[None]
    hp = x_prompt
    hs = x_sample.reshape(n_dec, D_MODEL)
    pool_t = jnp.transpose(state_pool, (0, 2, 1, 3))
    conv_t = jnp.transpose(state_conv, (0, 2, 1, 3))
    kt = jnp.transpose(state_k_win, (0, 2, 3, 1)).reshape(n_dec, KV_DIM, WINDOW)
    vt = jnp.transpose(state_v_win, (0, 2, 3, 1)).reshape(n_dec, KV_DIM, WINDOW)

    new_pool_p, new_pool_t, new_conv_p, new_conv_t = [], [], [], []
    for l in range(DEPTH):
        if l < N_A:
            emit_kv = l == N_A - 1
            new_pool_p.append(hp[:, S - POOL_STATE:])
            new_pool_t.append(jnp.concatenate([pool_t[l, 1:], hs[None]], axis=0))
            outs_m = _prompt_layer(hm, layer=l, kind="pool", is_meta=True, emit_kv=emit_kv, tile=BLOCK,
                                   name=f"meta_l{l}", halo_in=zeros_halo, gcar_in=zeros_gcar,
                                   pool=pool, ln=ln, ffn=ffn, kv=kvw)
            outs_p = _prompt_layer(hp, layer=l, kind="pool", is_meta=False, emit_kv=emit_kv, tile=ROW_TILE,
                                   name=f"prompt_l{l}", halo_in=hm[0, PAD_ROWS:], gcar_in=outs_m[1][0],
                                   pool=pool, ln=ln, ffn=ffn, kv=kvw)
            outs_s = _sample_layer(hs, layer=l, kind="pool", emit_kv=emit_kv, emit_q=emit_kv,
                                   name=f"sample_l{l}", mixer=(pool_t, *pool), ln=ln, ffn=ffn, conv_t=conv_t,
                                   kv=(wkv_b, bkv), qproj=(wq_b, bq))
            if emit_kv:
                kd_m, vd_m = outs_m[2][0], outs_m[3][0]
                kd_p, vd_p, kvtail = outs_p[2], outs_p[3], outs_p[4]
                kv_s, q_s = outs_s[2], outs_s[3]
        else:
            attn_w = (masks, attn_sinks, wq_b, bq, wao_b, bao)
            outs_m = _prompt_layer(hm, layer=l, kind="attn", is_meta=True, emit_kv=False, tile=BLOCK,
                                   name=f"meta_l{l}", gcar_in=zeros_gcar,
                                   attn=(kd_m[None], vd_m[None], zeros_kv, zeros_kv, *attn_w), ln=ln, ffn=ffn)
            outs_p = _prompt_layer(hp, layer=l, kind="attn", is_meta=False, emit_kv=False, tile=ROW_TILE,
                                   name=f"prompt_l{l}", gcar_in=outs_m[1][0],
                                   attn=(kd_p, vd_p, kd_m, vd_m, *attn_w), ln=ln, ffn=ffn)
            qexp = _expand_q(q_s.astype(BF16))
            if l == N_A:
                cols = lambda a: jnp.transpose(a.reshape(n_dec // SAMPLE_GROUP, SAMPLE_GROUP, KV_DIM), (0, 2, 1))
                res, kt, vt = _sample_attn(qexp, kt, vt, sinkb, l, f"sample_attn_l{l}",
                                           new_cols=(cols(kv_s[:, :KV_DIM]), cols(kv_s[:, KV_DIM:])))
            else:
                res = _sample_attn(qexp, kt, vt, sinkb, l, f"sample_attn_l{l}")[0]
            emit_q = l + 1 < DEPTH
            outs_s = _sample_layer(hs, layer=l, kind="attn", emit_kv=False, emit_q=emit_q, name=f"sample_l{l}",
                                   mixer=(_contract_o(res), wao_b, bao), ln=ln, ffn=ffn, conv_t=conv_t,
                                   qproj=(wq_b, bq))
            if emit_q:
                q_s = outs_s[2]
        new_conv_p.append(outs_p[1][:, SUBLANES - (CONV_W - 1):])
        new_conv_t.append(jnp.stack([conv_t[l, 1], outs_s[1]]))
        hm, hp, hs = outs_m[0], outs_p[0], outs_s[0]

    unwin = lambda t: jnp.transpose(t.reshape(-1, N_KV_HEADS, HEAD_DIM, WINDOW), (0, 3, 1, 2))
    new_k_p = kvtail[:, :, :KV_DIM].reshape(B, WINDOW, N_KV_HEADS, HEAD_DIM)
    new_v_p = kvtail[:, :, KV_DIM:].reshape(B, WINDOW, N_KV_HEADS, HEAD_DIM)
    return (hp, hs.reshape(n_dec, 1, D_MODEL),
            jnp.stack(new_pool_p), jnp.transpose(jnp.stack(new_pool_t), (0, 2, 1, 3)),
            jnp.stack(new_conv_p), jnp.transpose(jnp.stack(new_conv_t), (0, 2, 1, 3)),
            new_k_p, new_v_p, unwin(kt), unwin(vt))
```

```python
import functools

import jax
import jax.numpy as jnp
from jax import lax
from jax.experimental import pallas as pl
from jax.experimental.pallas import tpu as pltpu

F32 = jnp.float32
BF16 = jnp.bfloat16

D_MODEL = 1024
DEPTH = 4
N_META = 16
N_A = DEPTH // 2
POOL_WINDOWS = (2, 4, 8, 16)
N_POOL_GROUPS = len(POOL_WINDOWS)
POOL_GC = D_MODEL // N_POOL_GROUPS
MAX_POOL = max(POOL_WINDOWS)
POOL_STATE = MAX_POOL - 1
HEAD_DIM = 64
N_HEADS = D_MODEL // HEAD_DIM
N_KV_HEADS = 4
GROUP = N_HEADS // N_KV_HEADS
KV_DIM = N_KV_HEADS * HEAD_DIM
WINDOW = 128
BLOCK = 128
ATTN_SCALE = HEAD_DIM ** -0.5
D_FF = 2816
CONV_W = 3
ALPHA = (2.0 * DEPTH) ** 0.25
LN_EPS = 1e-5
NEG = -1e30

SUBLANES = 8
LANES = 128
PAD_ROWS = BLOCK - N_META
FF_CHUNK = 256
N_FF_CHUNKS = D_FF // FF_CHUNK
OUT_BLOCK = 256
N_OUT_BLOCKS = D_MODEL // OUT_BLOCK
DOWN_ROWS = 128
GATE_ROWS = 128
ROW_TILE = 512
SAMPLE_GROUP = 16
DUP = 2 * HEAD_DIM
KV_DUP = N_KV_HEADS * DUP
VMEM_LIMIT = 58 * 1024 * 1024


def _layer_norm(z, g, b):
    mu = jnp.mean(z, axis=-1, keepdims=True)
    zc = z - mu
    var = jnp.mean(zc * zc, axis=-1, keepdims=True)
    return zc * lax.rsqrt(var + LN_EPS) * g + b


def _silu(c):
    return c * (1.0 / (1.0 + jnp.exp(-c)))


def _dot(a, b):
    return jnp.dot(a, b, preferred_element_type=F32)


def _const_spec(a):
    nd = a.ndim
    return pl.BlockSpec(a.shape, lambda *_: (0,) * nd, pipeline_mode=pl.Buffered(1))


def _layer_spec(a, l):
    nd = a.ndim
    return pl.BlockSpec((None,) + a.shape[1:], lambda *_: (l,) + (0,) * (nd - 1),
                        pipeline_mode=pl.Buffered(1))


def _prompt_kernel(*refs, kind, is_meta, emit_kv, tile, n_tiles, n_steps, attn_idx):
    it = iter(refs)
    x_ref = next(it)
    halo_in_ref = next(it) if kind == "pool" else None
    gcar_in_ref = next(it)
    if kind == "pool":
        pw_ref, ps_ref = next(it), next(it)
    else:
        kcur_ref, vcur_ref, kprev_ref, vprev_ref, kmeta_ref, vmeta_ref = (next(it) for _ in range(6))
        mask_ref, sinks_ref = next(it), next(it)
        wq_ref, bq_ref, wao_ref, bao_ref = (next(it) for _ in range(4))
    lmg_ref, lmb_ref, lfg_ref, lfb_ref = (next(it) for _ in range(4))
    wi_ref, cw_ref, cb_ref, wo_ref = (next(it) for _ in range(4))
    if emit_kv:
        wkvd_ref, bkvd_ref, wkv_ref, bkv_ref = (next(it) for _ in range(4))
    y_ref, gtail_ref = next(it), next(it)
    if emit_kv:
        kd_ref, vd_ref, kvtail_ref = next(it), next(it), next(it)
    gcar_ref, hh_ref, h1_even_ref, h1_odd_ref, hb_ref = (next(it) for _ in range(5))
    if kind == "pool":
        halo_ref = next(it)
    else:
        qb_ref, ob_ref = next(it), next(it)
    if emit_kv:
        h2b_ref = next(it)

    T = tile
    n_rb = T // BLOCK
    s = pl.program_id(0)
    i = s % n_tiles
    has_cur = s < n_steps - 1

    @pl.when(s == 0)
    def _():
        hh_ref[...] = jnp.zeros_like(hh_ref)
        h1_even_ref[...] = jnp.zeros_like(h1_even_ref)
        h1_odd_ref[...] = jnp.zeros_like(h1_odd_ref)

    @pl.when(jnp.logical_and(has_cur, i == 0))
    def _():
        gcar_ref[...] = gcar_in_ref[...]
        if kind == "pool":
            halo_ref[...] = halo_in_ref[...]

    def block_rows(r):
        return slice(r * BLOCK, (r + 1) * BLOCK)

    def live_rows(r):
        return lax.broadcasted_iota(jnp.int32, (BLOCK, 1), 0) + r * BLOCK >= PAD_ROWS

    down_rows = min(DOWN_ROWS, T) if kind == "pool" else T
    down_ids = [(n, m) for m in range(T // down_rows) for n in range(N_OUT_BLOCKS)]

    def down_piece(f_pieces, n, m):
        f_pieces[n, m] = _dot(hh_ref[m * down_rows:(m + 1) * down_rows, :],
                              wo_ref[:, n * OUT_BLOCK:(n + 1) * OUT_BLOCK])

    def finish_rows(r, f_pieces, h1_old_ref):
        rows = block_rows(r)
        m, off = divmod(r * BLOCK, down_rows)
        f = jnp.concatenate([f_pieces[n, m][off:off + BLOCK] for n in range(N_OUT_BLOCKS)], axis=1)
        h2 = _layer_norm(ALPHA * h1_old_ref[rows, :] + f, lfg_ref[...], lfb_ref[...])
        y_ref[0, rows, :] = h2
        if emit_kv:
            h2b_ref[rows, :] = h2.astype(BF16)

    def kv_block(n):
        cols = slice(n * OUT_BLOCK, (n + 1) * OUT_BLOCK)
        kvd = (_dot(h2b_ref[...], wkvd_ref[:, cols]) + bkvd_ref[:, cols]).astype(BF16)
        if n * OUT_BLOCK < KV_DUP:
            kd_ref[0, :, cols] = kvd
        else:
            vd_ref[0, :, n * OUT_BLOCK - KV_DUP:(n + 1) * OUT_BLOCK - KV_DUP] = kvd

    def first_norm(r, x_blk, mix, h1_new_ref):
        rows = block_rows(r)
        h1 = _layer_norm(ALPHA * x_blk + mix, lmg_ref[...], lmb_ref[...])
        h1_new_ref[rows, :] = h1
        hb_ref[rows, :] = h1.astype(BF16)

    def pool_rows(r, h1_new_ref):
        rows = block_rows(r)
        x_blk = x_ref[0, rows, :]
        if is_meta:
            x_blk = jnp.where(live_rows(r), x_blk, 0.0)
        if r == 0:
            xe = jnp.concatenate([halo_ref[...], x_blk], axis=0)
        else:
            xe = x_ref[0, r * BLOCK - MAX_POOL:(r + 1) * BLOCK, :]
        s2 = xe + pltpu.roll(xe, 1, 0)
        s2r = s2[:, POOL_GC:]
        s4 = s2r + pltpu.roll(s2r, 2, 0)
        s4r = s4[:, POOL_GC:]
        s8 = s4r + pltpu.roll(s4r, 4, 0)
        s8r = s8[:, POOL_GC:]
        s16 = s8r + pltpu.roll(s8r, 8, 0)
        wins = (s2[MAX_POOL:, :POOL_GC], s4[MAX_POOL:, :POOL_GC],
                s8[MAX_POOL:, :POOL_GC], s16[MAX_POOL:])
        parts = []
        for gi, w in enumerate(POOL_WINDOWS):
            xg = x_blk[:, gi * POOL_GC:(gi + 1) * POOL_GC]
            if is_meta:
                t1 = lax.broadcasted_iota(jnp.int32, (BLOCK, 1), 0) + (r * BLOCK - PAD_ROWS + 1)
                dg = wins[gi] / jnp.clip(t1, 1, w).astype(F32) - xg
            else:
                dg = wins[gi] * (1.0 / w) - xg
            parts.append(_dot(dg.astype(BF16), pw_ref[gi]))
        mix = jnp.concatenate(parts, axis=1) * ps_ref[...]
        first_norm(r, x_blk, mix, h1_new_ref)

    def q_rows(r):
        rows = block_rows(r)
        qb_ref[rows, :] = (_dot(x_ref[0, rows, :].astype(BF16), wq_ref[...]) + bq_ref[...]).astype(BF16)

    lo = lax.broadcasted_iota(jnp.int32, (BLOCK, LANES), 1) < HEAD_DIM

    def keys_values(j, h):
        cols = slice(h * DUP, (h + 1) * DUP)
        if j == 0:
            first = i == 0
            kp = jnp.where(first, kmeta_ref[:, cols], kprev_ref[0, :, cols])
            vp = jnp.where(first, vmeta_ref[:, cols], vprev_ref[0, :, cols])
            return (jnp.concatenate([kp, kcur_ref[0, 0:BLOCK, cols]], axis=0),
                    jnp.concatenate([vp, vcur_ref[0, 0:BLOCK, cols]], axis=0))
        return (kcur_ref[0, (j - 1) * BLOCK:(j + 1) * BLOCK, cols],
                vcur_ref[0, (j - 1) * BLOCK:(j + 1) * BLOCK, cols])

    def scores(j, h):
        rows = block_rows(j)
        c0 = h * GROUP * HEAD_DIM
        qa = qb_ref[rows, c0:c0 + LANES]
        qc = qb_ref[rows, c0 + LANES:c0 + 2 * LANES]
        z = jnp.zeros_like(qa)
        lhs = jnp.concatenate([jnp.where(lo, qa, z), jnp.where(lo, z, qa),
                               jnp.where(lo, qc, z), jnp.where(lo, z, qc)], axis=0)
        kk, vv = keys_values(j, h)
        sc = lax.dot_general(lhs, kk, (((1,), (1,)), ((), ())), preferred_element_type=F32)
        return sc * ATTN_SCALE, vv

    def softmax(j, h, sc):
        if is_meta:
            madd = mask_ref[2]
        elif j == 0:
            madd = mask_ref[jnp.where(i == 0, 1, 0)]
        else:
            madd = mask_ref[0]
        ps = []
        for g in range(GROUP):
            sg = sc[g * BLOCK:(g + 1) * BLOCK] + madd
            sink = sinks_ref[attn_idx, h * GROUP + g]
            m = jnp.maximum(jnp.max(sg, axis=-1, keepdims=True), sink)
            e = jnp.exp(sg - m)
            den = jnp.sum(e, axis=-1, keepdims=True) + jnp.exp(sink - m)
            ps.append((e / den).astype(BF16))
        return jnp.concatenate(ps, axis=0)

    def weighted_values(j, h, p, vv):
        rows = block_rows(j)
        c0 = h * GROUP * HEAD_DIM
        res = _dot(p, vv)
        oa = jnp.where(lo, res[0:BLOCK], res[BLOCK:2 * BLOCK])
        oc = jnp.where(lo, res[2 * BLOCK:3 * BLOCK], res[3 * BLOCK:])
        ob_ref[rows, c0:c0 + LANES] = oa.astype(BF16)
        ob_ref[rows, c0 + LANES:c0 + 2 * LANES] = oc.astype(BF16)

    def attn_out_rows(r, h1_new_ref):
        rows = block_rows(r)
        mix = _dot(ob_ref[rows, :], wao_ref[...]) + bao_ref[...]
        first_norm(r, x_ref[0, rows, :], mix, h1_new_ref)

    gu_rows = min(GATE_ROWS, T)
    gu_ids = [(c, m) for c in range(N_FF_CHUNKS) for m in range(T // gu_rows)]

    def gate_up(c, m):
        hb = hb_ref[m * gu_rows:(m + 1) * gu_rows, :]
        return (_dot(hb, wi_ref[:, c * FF_CHUNK:(c + 1) * FF_CHUNK]),
                _dot(hb, wi_ref[:, D_FF + c * FF_CHUNK:D_FF + (c + 1) * FF_CHUNK]))

    def conv_rows(c, m, g, u):
        cols = slice(c * FF_CHUNK, (c + 1) * FF_CHUNK)
        if is_meta:
            g = jnp.where(jnp.concatenate([live_rows(r) for r in range(n_rb)], axis=0), g, 0.0)
        gext = jnp.concatenate([gcar_ref[:, cols], g], axis=0)
        g1 = pltpu.roll(gext, 1, 0)[SUBLANES:]
        g2 = pltpu.roll(gext, 2, 0)[SUBLANES:]
        gcar_ref[:, cols] = g[gu_rows - SUBLANES:]
        cc = (cb_ref[:, cols] + g2 * cw_ref[0:1, cols] + g1 * cw_ref[1:2, cols]
              + g * cw_ref[2:3, cols])
        hh_ref[m * gu_rows:(m + 1) * gu_rows, cols] = (_silu(cc) * u).astype(BF16)

    def body(h1_new_ref, h1_old_ref):
        f_pieces = {}
        downs = [functools.partial(down_piece, f_pieces, n, m) for n, m in down_ids]
        if kind == "pool":
            per = -(-len(downs) // n_rb)
            for r in range(n_rb):
                for _ in range(min(per, len(downs))):
                    downs.pop(0)()
                pool_rows(r, h1_new_ref)
            tail = x_ref[0, T - MAX_POOL:, :]
            if is_meta:
                tail = jnp.where(live_rows(n_rb - 1)[BLOCK - MAX_POOL:], tail, 0.0)
            halo_ref[...] = tail
        else:
            for r in range(n_rb):
                q_rows(r)
            units = [(j, h) for j in range(n_rb) for h in range(N_KV_HEADS)]
            per = -(-len(downs) // len(units))
            stride = max(len(units) // len(downs), 1)
            nxt = scores(*units[0])
            for p, (j, h) in enumerate(units):
                sc, vv = nxt
                if p + 1 < len(units):
                    nxt = scores(*units[p + 1])
                if p % stride == min(1, stride - 1):
                    for _ in range(min(per, len(downs))):
                        downs.pop(0)()
                weighted_values(j, h, softmax(j, h, sc), vv)
                if h == N_KV_HEADS - 1:
                    attn_out_rows(j, h1_new_ref)
        for job in downs:
            job()

        under = [functools.partial(finish_rows, r, f_pieces, h1_old_ref) for r in range(n_rb)]
        if emit_kv:
            under += [functools.partial(kv_block, n) for n in range(2 * KV_DUP // OUT_BLOCK)]
        nxt = gate_up(*gu_ids[0])
        for k, (c, m) in enumerate(gu_ids):
            g, u = nxt
            if k + 1 < len(gu_ids):
                nxt = gate_up(*gu_ids[k + 1])
            conv_rows(c, m, g, u)
            if under and (k + 1) % (T // gu_rows) == 0:
                under.pop(0)()
        for job in under:
            job()
        gtail_ref[0] = gcar_ref[...]

    @pl.when(jnp.logical_and(has_cur, s % 2 == 0))
    def _():
        body(h1_even_ref, h1_odd_ref)

    @pl.when(jnp.logical_and(has_cur, s % 2 == 1))
    def _():
        body(h1_odd_ref, h1_even_ref)

    @pl.when(s == n_steps - 1)
    def _():
        h1_old_ref = h1_even_ref if (n_steps - 2) % 2 == 0 else h1_odd_ref
        f_pieces = {}
        for n, m in down_ids:
            down_piece(f_pieces, n, m)
        for r in range(n_rb):
            finish_rows(r, f_pieces, h1_old_ref)
        if emit_kv:
            for n in range(2 * KV_DUP // OUT_BLOCK):
                kv_block(n)

    if emit_kv:
        @pl.when(jnp.logical_and(s > 0, (s - 1) % n_tiles == n_tiles - 1))
        def _():
            kvtail_ref[0] = _dot(h2b_ref[T - WINDOW:, :], wkv_ref[...]) + bkv_ref[...]


def _prompt_layer(x, *, layer, kind, is_meta, emit_kv, tile, name, halo_in=None, gcar_in,
                  pool=None, attn=None, ln, ffn, kv=None):
    B, L, _ = x.shape
    n_tiles = L // tile
    n_all = B * n_tiles
    n_steps = n_all + 1
    nb = tile // BLOCK
    attn_idx = layer - N_A

    def cur(s):
        t = jnp.minimum(s, n_all - 1)
        return t // n_tiles, t % n_tiles

    def prev(s):
        t = jnp.maximum(s - 1, 0)
        return t // n_tiles, t % n_tiles

    cur_rows = lambda w: pl.BlockSpec((1, tile, w), lambda s: (*cur(s), 0))
    prev_rows = lambda w: pl.BlockSpec((1, tile, w), lambda s: (*prev(s), 0))
    ins, specs = [x], [cur_rows(D_MODEL)]

    def add(a, spec):
        ins.append(a)
        specs.append(spec)

    if kind == "pool":
        add(halo_in, _const_spec(halo_in))
    add(gcar_in, _const_spec(gcar_in))
    if kind == "pool":
        for a in pool:
            add(a, _layer_spec(a, layer))
    else:
        kd, vd, kmeta, vmeta, mask, sinks, wq, bq, wao, bao = attn

        def before(s):
            b, i = cur(s)
            return b, jnp.maximum(i * nb - 1, 0), 0

        before_spec = pl.BlockSpec((1, BLOCK, KV_DUP), before)
        ins += [kd, vd, kd, vd]
        specs += [cur_rows(KV_DUP), cur_rows(KV_DUP), before_spec, before_spec]
        for a in (kmeta, vmeta, mask):
            add(a, _const_spec(a))
        add(sinks, pl.BlockSpec(memory_space=pltpu.SMEM))
        for a in (wq, bq, wao, bao):
            add(a, _layer_spec(a, attn_idx))
    for a in (*ln, *ffn):
        add(a, _layer_spec(a, layer))
    if emit_kv:
        for a in kv:
            add(a, _const_spec(a))

    out_shape = [jax.ShapeDtypeStruct((B, L, D_MODEL), F32),
                 jax.ShapeDtypeStruct((B, SUBLANES, D_FF), F32)]
    out_specs = [prev_rows(D_MODEL), pl.BlockSpec((1, SUBLANES, D_FF), lambda s: (cur(s)[0], 0, 0))]
    if emit_kv:
        out_shape += [jax.ShapeDtypeStruct((B, L, KV_DUP), BF16),
                      jax.ShapeDtypeStruct((B, L, KV_DUP), BF16),
                      jax.ShapeDtypeStruct((B, WINDOW, 2 * KV_DIM), F32)]
        out_specs += [prev_rows(KV_DUP), prev_rows(KV_DUP),
                      pl.BlockSpec((1, WINDOW, 2 * KV_DIM), lambda s: (prev(s)[0], 0, 0))]
    scratch = [pltpu.VMEM((SUBLANES, D_FF), F32), pltpu.VMEM((tile, D_FF), BF16),
               pltpu.VMEM((tile, D_MODEL), F32), pltpu.VMEM((tile, D_MODEL), F32),
               pltpu.VMEM((tile, D_MODEL), BF16)]
    if kind == "pool":
        scratch.append(pltpu.VMEM((MAX_POOL, D_MODEL), F32))
    else:
        scratch += [pltpu.VMEM((tile, D_MODEL), BF16), pltpu.VMEM((tile, D_MODEL), BF16)]
    if emit_kv:
        scratch.append(pltpu.VMEM((tile, D_MODEL), BF16))
    body = functools.partial(_prompt_kernel, kind=kind, is_meta=is_meta, emit_kv=emit_kv,
                             tile=tile, n_tiles=n_tiles, n_steps=n_steps, attn_idx=attn_idx)
    return pl.pallas_call(
        body,
        grid=(n_steps,),
        in_specs=specs,
        out_specs=out_specs,
        out_shape=out_shape,
        scratch_shapes=scratch,
        compiler_params=pltpu.CompilerParams(
            dimension_semantics=("arbitrary",), vmem_limit_bytes=VMEM_LIMIT),
        name=name,
    )(*ins)


def _sample_kernel(*refs, kind, emit_kv, emit_q):
    it = iter(refs)
    x_ref = next(it)
    if kind == "pool":
        sp_ref, pw_ref, ps_ref = next(it), next(it), next(it)
    else:
        o_ref, wao_ref, bao_ref = next(it), next(it), next(it)
    lmg_ref, lmb_ref, lfg_ref, lfb_ref = (next(it) for _ in range(4))
    wg_ref, wu_ref, cw_ref, cb_ref, wo_ref, s0_ref, s1_ref = (next(it) for _ in range(7))
    if emit_kv:
        wkv_ref, bkv_ref = next(it), next(it)
    if emit_q:
        wq_ref, bq_ref = next(it), next(it)
    y_ref, g_ref = next(it), next(it)
    if emit_kv:
        kv_ref = next(it)
    if emit_q:
        q_ref = next(it)
    h1_ref, h1b_ref, acc_ref = next(it), next(it), next(it)

    c = pl.program_id(0)

    @pl.when(c == 0)
    def _():
        x = x_ref[...]
        if kind == "pool":
            parts = []
            for gi, w in enumerate(POOL_WINDOWS):
                cols = slice(gi * POOL_GC, (gi + 1) * POOL_GC)
                win = x[:, cols]
                for k in range(1, w):
                    win = win + sp_ref[POOL_STATE - k, :, cols]
                dg = win * (1.0 / w) - x[:, cols]
                parts.append(_dot(dg.astype(BF16), pw_ref[gi]))
            mix = jnp.concatenate(parts, axis=1) * ps_ref[...]
        else:
            mix = _dot(o_ref[...], wao_ref[...]) + bao_ref[...]
        h1 = _layer_norm(ALPHA * x + mix, lmg_ref[...], lmb_ref[...])
        h1_ref[...] = h1
        h1b_ref[...] = h1.astype(BF16)
        acc_ref[...] = jnp.zeros_like(acc_ref)

    h1b = h1b_ref[...]
    g = _dot(h1b, wg_ref[...])
    u = _dot(h1b, wu_ref[...])
    g_ref[...] = g
    cc = cb_ref[...] + s0_ref[...] * cw_ref[0:1, :] + s1_ref[...] * cw_ref[1:2, :] + g * cw_ref[2:3, :]
    acc_ref[...] += _dot((_silu(cc) * u).astype(BF16), wo_ref[...])

    @pl.when(c == N_FF_CHUNKS - 1)
    def _():
        h2 = _layer_norm(ALPHA * h1_ref[...] + acc_ref[...], lfg_ref[...], lfb_ref[...])
        y_ref[...] = h2
        h2b = h2.astype(BF16)
        if emit_kv:
            kv_ref[...] = _dot(h2b, wkv_ref[...]) + bkv_ref[...]
        if emit_q:
            q_ref[...] = _dot(h2b, wq_ref[...]) + bq_ref[...]


def _sample_layer(x, *, layer, kind, emit_kv, emit_q, name, mixer, ln, ffn, conv_t, kv=None, qproj=None):
    n = x.shape[0]
    ins, specs = [x], [_const_spec(x)]

    def add(a, spec):
        ins.append(a)
        specs.append(spec)

    if kind == "pool":
        for a in mixer:
            add(a, _layer_spec(a, layer))
    else:
        o, wao, bao = mixer
        add(o, _const_spec(o))
        add(wao, _layer_spec(wao, layer - N_A))
        add(bao, _layer_spec(bao, layer - N_A))
    for a in ln:
        add(a, _layer_spec(a, layer))
    wi, cw, cb, wo = ffn
    add(wi, pl.BlockSpec((None, D_MODEL, FF_CHUNK), lambda c: (layer, 0, c)))
    add(wi, pl.BlockSpec((None, D_MODEL, FF_CHUNK), lambda c: (layer, 0, N_FF_CHUNKS + c)))
    add(cw, pl.BlockSpec((None, CONV_W, FF_CHUNK), lambda c: (layer, 0, c)))
    add(cb, pl.BlockSpec((None, 1, FF_CHUNK), lambda c: (layer, 0, c)))
    add(wo, pl.BlockSpec((None, FF_CHUNK, D_MODEL), lambda c: (layer, c, 0)))
    add(conv_t, pl.BlockSpec((None, None, n, FF_CHUNK), lambda c: (layer, 0, 0, c)))
    add(conv_t, pl.BlockSpec((None, None, n, FF_CHUNK), lambda c: (layer, 1, 0, c)))
    if emit_kv:
        for a in kv:
            add(a, _const_spec(a))
    if emit_q:
        for a in qproj:
            add(a, _layer_spec(a, layer + 1 - N_A))

    full = lambda w: pl.BlockSpec((n, w), lambda c: (0, 0))
    out_shape = [jax.ShapeDtypeStruct((n, D_MODEL), F32), jax.ShapeDtypeStruct((n, D_FF), F32)]
    out_specs = [full(D_MODEL), pl.BlockSpec((n, FF_CHUNK), lambda c: (0, c))]
    if emit_kv:
        out_shape.append(jax.ShapeDtypeStruct((n, 2 * KV_DIM), F32))
        out_specs.append(full(2 * KV_DIM))
    if emit_q:
        out_shape.append(jax.ShapeDtypeStruct((n, D_MODEL), F32))
        out_specs.append(full(D_MODEL))
    body = functools.partial(_sample_kernel, kind=kind, emit_kv=emit_kv, emit_q=emit_q)
    return pl.pallas_call(
        body,
        grid=(N_FF_CHUNKS,),
        in_specs=specs,
        out_specs=out_specs,
        out_shape=out_shape,
        scratch_shapes=[pltpu.VMEM((n, D_MODEL), F32), pltpu.VMEM((n, D_MODEL), BF16),
                        pltpu.VMEM((n, D_MODEL), F32)],
        compiler_params=pltpu.CompilerParams(dimension_semantics=("arbitrary",),
                                             vmem_limit_bytes=VMEM_LIMIT),
        name=name,
    )(*ins)


def _sample_attn_kernel(*refs, update):
    it = iter(refs)
    q_ref, kt_ref, vt_ref = next(it), next(it), next(it)
    if update:
        kn_ref, vn_ref = next(it), next(it)
    sink_ref, o_ref = next(it), next(it)
    if update:
        nkt_ref, nvt_ref = next(it), next(it)

    kt, vt = kt_ref[...], vt_ref[...]
    if update:
        newest = lax.broadcasted_iota(jnp.int32, (KV_DIM, WINDOW), 1) == WINDOW - 1

        def advance(win, new_ref):
            return jnp.stack([jnp.where(newest, new_ref[:, b:b + 1], pltpu.roll(win[b], WINDOW - 1, 1))
                              for b in range(SAMPLE_GROUP)])

        kt, vt = advance(kt, kn_ref), advance(vt, vn_ref)
        nkt_ref[...] = kt
        nvt_ref[...] = vt
    q = q_ref[...]
    s = jnp.einsum("bhc,bcj->bhj", q, kt.astype(BF16), preferred_element_type=F32) * ATTN_SCALE
    sink = sink_ref[...][None]
    m = jnp.max(jnp.maximum(s, sink), axis=-1, keepdims=True)
    e = jnp.exp(s - m)
    den = jnp.sum(e, axis=-1, keepdims=True) + jnp.exp(sink - m)
    p = (e / den).astype(BF16)
    o_ref[...] = jnp.einsum("bhj,bcj->bhc", p, vt.astype(BF16), preferred_element_type=F32).astype(BF16)


def _sample_attn(qexp, kt, vt, sinkb, layer, name, new_cols=None):
    n = qexp.shape[0]
    G = SAMPLE_GROUP
    update = new_cols is not None
    win_spec = pl.BlockSpec((G, KV_DIM, WINDOW), lambda i: (i, 0, 0))
    q_spec = pl.BlockSpec((G, N_HEADS, KV_DIM), lambda i: (i, 0, 0))
    ins, specs = [qexp, kt, vt], [q_spec, win_spec, win_spec]
    if update:
        col_spec = pl.BlockSpec((None, KV_DIM, G), lambda i: (i, 0, 0))
        ins += list(new_cols)
        specs += [col_spec, col_spec]
    ins.append(sinkb)
    specs.append(pl.BlockSpec((None, N_HEADS, WINDOW), lambda i: (layer - N_A, 0, 0)))
    out_shape = [jax.ShapeDtypeStruct((n, N_HEADS, KV_DIM), BF16)]
    out_specs = [q_spec]
    if update:
        out_shape += [jax.ShapeDtypeStruct(kt.shape, F32)] * 2
        out_specs += [win_spec, win_spec]
    return pl.pallas_call(
        functools.partial(_sample_attn_kernel, update=update),
        grid=(n // G,),
        in_specs=specs,
        out_specs=out_specs,
        out_shape=out_shape,
        compiler_params=pltpu.CompilerParams(dimension_semantics=("arbitrary",)),
        name=name,
    )(*ins)


def _dup_heads(w):
    lead = w.shape[:-1]
    w = w.reshape(*lead, N_KV_HEADS, 1, HEAD_DIM)
    return jnp.broadcast_to(w, (*lead, N_KV_HEADS, 2, HEAD_DIM)).reshape(*lead, KV_DUP)


def _expand_q(q):
    n = q.shape[0]
    q = q.reshape(n, N_KV_HEADS, GROUP, HEAD_DIM)
    parts = [jnp.pad(q[:, h], ((0, 0), (0, 0), (h * HEAD_DIM, KV_DIM - (h + 1) * HEAD_DIM)))
             for h in range(N_KV_HEADS)]
    return jnp.stack(parts, axis=1).reshape(n, N_HEADS, KV_DIM)


def _contract_o(res):
    n = res.shape[0]
    res = res.reshape(n, N_KV_HEADS, GROUP, N_KV_HEADS, HEAD_DIM)
    return jnp.stack([res[:, h, :, h, :] for h in range(N_KV_HEADS)], axis=1).reshape(n, N_HEADS * HEAD_DIM)


def _band_masks():
    i = jnp.arange(BLOCK)[:, None]
    j = jnp.arange(2 * BLOCK)[None, :]
    band = (j > i) & (j <= i + BLOCK)
    masks = jnp.stack([band, band & (j >= PAD_ROWS), band & (j >= BLOCK + PAD_ROWS)])
    return jnp.where(masks, 0.0, NEG).astype(F32)


def kernel(x_prompt, x_sample, state_pool, state_conv, state_k_win, state_v_win, meta_tokens, pool_w, pool_scale, w_kv, b_kv, attn_w_q, attn_b_q, attn_sinks, attn_w_o, attn_b_o, ffn_w_in, ffn_conv_w, ffn_conv_b, ffn_w_out, ln_mix_g, ln_mix_b, ln_ffn_g, ln_ffn_b):
    B, S, _ = x_prompt.shape
    n_dec = x_sample.shape[0]
    assert x_sample.shape[1] == 1 and S % ROW_TILE == 0 and n_dec % SAMPLE_GROUP == 0

    row = lambda a: a[:, None, :]
    ffn = (ffn_w_in.astype(BF16), ffn_conv_w, row(ffn_conv_b), ffn_w_out.astype(BF16))
    ln = (row(ln_mix_g), row(ln_mix_b), row(ln_ffn_g), row(ln_ffn_b))
    pool = (pool_w.astype(BF16), row(pool_scale))
    wkv_b, bkv = w_kv.astype(BF16), b_kv[None, :]
    wkvd_b = jnp.concatenate([_dup_heads(w_kv[:, :KV_DIM]), _dup_heads(w_kv[:, KV_DIM:])], axis=1).astype(BF16)
    bkvd = jnp.concatenate([_dup_heads(b_kv[:KV_DIM]), _dup_heads(b_kv[KV_DIM:])])[None, :]
    kvw = (wkvd_b, bkvd, wkv_b, bkv)
    wq_b, bq = attn_w_q.astype(BF16), row(attn_b_q)
    wao_b, bao = attn_w_o.astype(BF16), row(attn_b_o)
    sinkb = jnp.broadcast_to(attn_sinks[:, :, None], (attn_sinks.shape[0], N_HEADS, WINDOW))
    masks = _band_masks()

    zeros_halo = jnp.zeros((MAX_POOL, D_MODEL), F32)
    zeros_gcar = jnp.zeros((SUBLANES, D_FF), F32)
    zeros_kv = jnp.zeros((BLOCK, KV_DUP), BF16)

    hm = jnp.concatenate([jnp.zeros((PAD_ROWS, D_MODEL), F32), meta_tokens.astype(F32)])[None]
    hp = x_prompt
    hs = x_sample.reshape(n_dec, D_MODEL)
    pool_t = jnp.transpose(state_pool, (0, 2, 1, 3))
    conv_t = jnp.transpose(state_conv, (0, 2, 1, 3))
    kt = jnp.transpose(state_k_win, (0, 2, 3, 1)).reshape(n_dec, KV_DIM, WINDOW)
    vt = jnp.transpose(state_v_win, (0, 2, 3, 1)).reshape(n_dec, KV_DIM, WINDOW)

    new_pool_p, new_pool_t, new_conv_p, new_conv_t = [], [], [], []
    for l in range(DEPTH):
        if l < N_A:
            emit_kv = l == N_A - 1
            new_pool_p.append(hp[:, S - POOL_STATE:])
            new_pool_t.append(jnp.concatenate([pool_t[l, 1:], hs[None]], axis=0))
            outs_m = _prompt_layer(hm, layer=l, kind="pool", is_meta=True, emit_kv=emit_kv, tile=BLOCK,
                                   name=f"meta_l{l}", halo_in=zeros_halo, gcar_in=zeros_gcar,
                                   pool=pool, ln=ln, ffn=ffn, kv=kvw)
            outs_p = _prompt_layer(hp, layer=l, kind="pool", is_meta=False, emit_kv=emit_kv, tile=ROW_TILE,
                                   name=f"prompt_l{l}", halo_in=hm[0, PAD_ROWS:], gcar_in=outs_m[1][0],
                                   pool=pool, ln=ln, ffn=ffn, kv=kvw)
            outs_s = _sample_layer(hs, layer=l, kind="pool", emit_kv=emit_kv, emit_q=emit_kv,
                                   name=f"sample_l{l}", mixer=(pool_t, *pool), ln=ln, ffn=ffn, conv_t=conv_t,
                                   kv=(wkv_b, bkv), qproj=(wq_b, bq))
            if emit_kv:
                kd_m, vd_m = outs_m[2][0], outs_m[3][0]
                kd_p, vd_p, kvtail = outs_p[2], outs_p[3], outs_p[4]
                kv_s, q_s = outs_s[2], outs_s[3]
        else:
            attn_w = (masks, attn_sinks, wq_b, bq, wao_b, bao)
            outs_m = _prompt_layer(hm, layer=l, kind="attn", is_meta=True, emit_kv=False, tile=BLOCK,
                                   name=f"meta_l{l}", gcar_in=zeros_gcar,
                                   attn=(kd_m[None], vd_m[None], zeros_kv, zeros_kv, *attn_w), ln=ln, ffn=ffn)
            outs_p = _prompt_layer(hp, layer=l, kind="attn", is_meta=False, emit_kv=False, tile=ROW_TILE,
                                   name=f"prompt_l{l}", gcar_in=outs_m[1][0],
                                   attn=(kd_p, vd_p, kd_m, vd_m, *attn_w), ln=ln, ffn=ffn)
            qexp = _expand_q(q_s.astype(BF16))
            if l == N_A:
                cols = lambda a: jnp.transpose(a.reshape(n_dec // SAMPLE_GROUP, SAMPLE_GROUP, KV_DIM), (0, 2, 1))
                res, kt, vt = _sample_attn(qexp, kt, vt, sinkb, l, f"sample_attn_l{l}",
                                           new_cols=(cols(kv_s[:, :KV_DIM]), cols(kv_s[:, KV_DIM:])))
            else:
                res = _sample_attn(qexp, kt, vt, sinkb, l, f"sample_attn_l{l}")[0]
            emit_q = l + 1 < DEPTH
            outs_s = _sample_layer(hs, layer=l, kind="attn", emit_kv=False, emit_q=emit_q, name=f"sample_l{l}",
                                   mixer=(_contract_o(res), wao_b, bao), ln=ln, ffn=ffn, conv_t=conv_t,
                                   qproj=(wq_b, bq))
            if emit_q:
                q_s = outs_s[2]
        new_conv_p.append(outs_p[1][:, SUBLANES - (CONV_W - 1):])
        new_conv_t.append(jnp.stack([conv_t[l, 1], outs_s[1]]))
        hm, hp, hs = outs_m[0], outs_p[0], outs_s[0]

    unwin = lambda t: jnp.transpose(t.reshape(-1, N_KV_HEADS, HEAD_DIM, WINDOW), (0, 3, 1, 2))
    new_k_p = kvtail[:, :, :KV_DIM].reshape(B, WINDOW, N_KV_HEADS, HEAD_DIM)
    new_v_p = kvtail[:, :, KV_DIM:].reshape(B, WINDOW, N_KV_HEADS, HEAD_DIM)
    return (hp, hs.reshape(n_dec, 1, D_MODEL),
            jnp.stack(new_pool_p), jnp.transpose(jnp.stack(new_pool_t), (0, 2, 1, 3)),
            jnp.stack(new_conv_p), jnp.transpose(jnp.stack(new_conv_t), (0, 2, 1, 3)),
            new_k_p, new_v_p, unwin(kt), unwin(vt))
```

```python
import functools

import jax
import jax.numpy as jnp
from jax import lax
from jax.experimental import pallas as pl
from jax.experimental.pallas import tpu as pltpu

F32 = jnp.float32
BF16 = jnp.bfloat16

D_MODEL = 1024
DEPTH = 4
N_META = 16
N_A = DEPTH // 2
POOL_WINDOWS = (2, 4, 8, 16)
N_POOL_GROUPS = len(POOL_WINDOWS)
POOL_GC = D_MODEL // N_POOL_GROUPS
MAX_POOL = max(POOL_WINDOWS)
POOL_STATE = MAX_POOL - 1
HEAD_DIM = 64
N_HEADS = D_MODEL // HEAD_DIM
N_KV_HEADS = 4
GROUP = N_HEADS // N_KV_HEADS
KV_DIM = N_KV_HEADS * HEAD_DIM
WINDOW = 128
BLOCK = 128
ATTN_SCALE = HEAD_DIM ** -0.5
D_FF = 2816
CONV_W = 3
ALPHA = (2.0 * DEPTH) ** 0.25
LN_EPS = 1e-5
NEG = -1e30

SUBLANES = 8
LANES = 128
PAD_ROWS = BLOCK - N_META
FF_CHUNK = 256
N_FF_CHUNKS = D_FF // FF_CHUNK
OUT_BLOCK = 256
N_OUT_BLOCKS = D_MODEL // OUT_BLOCK
DEC_CHUNK = D_FF // 2
N_DEC_CHUNKS = D_FF // DEC_CHUNK
CAST_SLABS = 16
DOWN_ROWS = 128
GATE_ROWS = 128
ROW_TILE = 512
SAMPLE_GROUP = 16
DUP = 2 * HEAD_DIM
KV_DUP = N_KV_HEADS * DUP
VMEM_LIMIT = 58 * 1024 * 1024


def _layer_norm(z, g, b):
    mu = jnp.mean(z, axis=-1, keepdims=True)
    zc = z - mu
    var = jnp.mean(zc * zc, axis=-1, keepdims=True)
    return zc * lax.rsqrt(var + LN_EPS) * g + b


def _silu(c):
    return c * (1.0 / (1.0 + jnp.exp(-c)))


def _dot(a, b):
    return jnp.dot(a, b, preferred_element_type=F32)


def _const_spec(a):
    nd = a.ndim
    return pl.BlockSpec(a.shape, lambda *_: (0,) * nd, pipeline_mode=pl.Buffered(1))


def _layer_spec(a, l):
    nd = a.ndim
    return pl.BlockSpec((None,) + a.shape[1:], lambda *_: (l,) + (0,) * (nd - 1),
                        pipeline_mode=pl.Buffered(1))


def _prompt_kernel(*refs, kind, is_meta, emit_kv, tile, n_tiles, n_steps, attn_idx, n_cast):
    it = iter(refs)
    x_ref = next(it)
    halo_in_ref = next(it) if kind == "pool" else None
    gcar_in_ref = next(it)
    if kind == "pool":
        pw_ref, ps_ref = next(it), next(it)
    else:
        kcur_ref, vcur_ref, kprev_ref, vprev_ref, kmeta_ref, vmeta_ref = (next(it) for _ in range(6))
        mask_ref, sinks_ref = next(it), next(it)
        wq_ref, bq_ref, wao_ref, bao_ref = (next(it) for _ in range(4))
    lmg_ref, lmb_ref, lfg_ref, lfb_ref = (next(it) for _ in range(4))
    wi_ref, cw_ref, cb_ref, wo_ref = (next(it) for _ in range(4))
    if emit_kv:
        wkvd_ref, bkvd_ref, wkv_ref, bkv_ref = (next(it) for _ in range(4))
    cast_in_refs = [next(it) for _ in range(n_cast)]
    y_ref, gtail_ref = next(it), next(it)
    if emit_kv:
        kd_ref, vd_ref, kvtail_ref = next(it), next(it), next(it)
    for src_ref in cast_in_refs:
        next(it)[...] = src_ref[...].astype(BF16)
    gcar_ref, hh_ref, h1_even_ref, h1_odd_ref, hb_ref = (next(it) for _ in range(5))
    if kind == "pool":
        halo_ref = next(it)
    else:
        qb_ref, ob_ref = next(it), next(it)
    if emit_kv:
        h2b_ref = next(it)

    T = tile
    n_rb = T // BLOCK
    s = pl.program_id(0)
    i = s % n_tiles
    has_cur = s < n_steps - 1

    @pl.when(s == 0)
    def _():
        hh_ref[...] = jnp.zeros_like(hh_ref)
        h1_even_ref[...] = jnp.zeros_like(h1_even_ref)
        h1_odd_ref[...] = jnp.zeros_like(h1_odd_ref)

    @pl.when(jnp.logical_and(has_cur, i == 0))
    def _():
        gcar_ref[...] = gcar_in_ref[...]
        if kind == "pool":
            halo_ref[...] = halo_in_ref[...]

    def block_rows(r):
        return slice(r * BLOCK, (r + 1) * BLOCK)

    def live_rows(r):
        return lax.broadcasted_iota(jnp.int32, (BLOCK, 1), 0) + r * BLOCK >= PAD_ROWS

    down_rows = min(DOWN_ROWS, T) if kind == "pool" else T
    down_ids = [(n, m) for m in range(T // down_rows) for n in range(N_OUT_BLOCKS)]

    def down_piece(f_pieces, n, m):
        f_pieces[n, m] = _dot(hh_ref[m * down_rows:(m + 1) * down_rows, :],
                              wo_ref[:, n * OUT_BLOCK:(n + 1) * OUT_BLOCK])

    def finish_rows(r, f_pieces, h1_old_ref):
        rows = block_rows(r)
        m, off = divmod(r * BLOCK, down_rows)
        f = jnp.concatenate([f_pieces[n, m][off:off + BLOCK] for n in range(N_OUT_BLOCKS)], axis=1)
        h2 = _layer_norm(ALPHA * h1_old_ref[rows, :] + f, lfg_ref[...], lfb_ref[...])
        y_ref[0, rows, :] = h2
        if emit_kv:
            h2b_ref[rows, :] = h2.astype(BF16)

    def kv_block(n):
        cols = slice(n * OUT_BLOCK, (n + 1) * OUT_BLOCK)
        kvd = (_dot(h2b_ref[...], wkvd_ref[:, cols]) + bkvd_ref[:, cols]).astype(BF16)
        if n * OUT_BLOCK < KV_DUP:
            kd_ref[0, :, cols] = kvd
        else:
            vd_ref[0, :, n * OUT_BLOCK - KV_DUP:(n + 1) * OUT_BLOCK - KV_DUP] = kvd

    def first_norm(r, x_blk, mix, h1_new_ref):
        rows = block_rows(r)
        h1 = _layer_norm(ALPHA * x_blk + mix, lmg_ref[...], lmb_ref[...])
        h1_new_ref[rows, :] = h1
        hb_ref[rows, :] = h1.astype(BF16)

    def pool_rows(r, h1_new_ref):
        rows = block_rows(r)
        x_blk = x_ref[0, rows, :]
        if is_meta:
            x_blk = jnp.where(live_rows(r), x_blk, 0.0)
        if r == 0:
            xe = jnp.concatenate([halo_ref[...], x_blk], axis=0)
        else:
            xe = x_ref[0, r * BLOCK - MAX_POOL:(r + 1) * BLOCK, :]
        s2 = xe + pltpu.roll(xe, 1, 0)
        s2r = s2[:, POOL_GC:]
        s4 = s2r + pltpu.roll(s2r, 2, 0)
        s4r = s4[:, POOL_GC:]
        s8 = s4r + pltpu.roll(s4r, 4, 0)
        s8r = s8[:, POOL_GC:]
        s16 = s8r + pltpu.roll(s8r, 8, 0)
        wins = (s2[MAX_POOL:, :POOL_GC], s4[MAX_POOL:, :POOL_GC],
                s8[MAX_POOL:, :POOL_GC], s16[MAX_POOL:])
        parts = []
        for gi, w in enumerate(POOL_WINDOWS):
            xg = x_blk[:, gi * POOL_GC:(gi + 1) * POOL_GC]
            if is_meta:
                t1 = lax.broadcasted_iota(jnp.int32, (BLOCK, 1), 0) + (r * BLOCK - PAD_ROWS + 1)
                dg = wins[gi] / jnp.clip(t1, 1, w).astype(F32) - xg
            else:
                dg = wins[gi] * (1.0 / w) - xg
            parts.append(_dot(dg.astype(BF16), pw_ref[gi]))
        mix = jnp.concatenate(parts, axis=1) * ps_ref[...]
        first_norm(r, x_blk, mix, h1_new_ref)

    def q_rows(r):
        rows = block_rows(r)
        q = (_dot(x_ref[0, rows, :].astype(BF16), wq_ref[...]) + bq_ref[...]) * ATTN_SCALE
        qb_ref[rows, :] = q.astype(BF16)

    lo = lax.broadcasted_iota(jnp.int32, (BLOCK, LANES), 1) < HEAD_DIM

    def keys_values(j, h):
        cols = slice(h * DUP, (h + 1) * DUP)
        if j == 0:
            first = i == 0
            kp = jnp.where(first, kmeta_ref[:, cols], kprev_ref[0, :, cols])
            vp = jnp.where(first, vmeta_ref[:, cols], vprev_ref[0, :, cols])
            return (jnp.concatenate([kp, kcur_ref[0, 0:BLOCK, cols]], axis=0),
                    jnp.concatenate([vp, vcur_ref[0, 0:BLOCK, cols]], axis=0))
        return (kcur_ref[0, (j - 1) * BLOCK:(j + 1) * BLOCK, cols],
                vcur_ref[0, (j - 1) * BLOCK:(j + 1) * BLOCK, cols])

    def scores(j, h):
        rows = block_rows(j)
        c0 = h * GROUP * HEAD_DIM
        qa = qb_ref[rows, c0:c0 + LANES]
        qc = qb_ref[rows, c0 + LANES:c0 + 2 * LANES]
        z = jnp.zeros_like(qa)
        lhs = jnp.concatenate([jnp.where(lo, qa, z), jnp.where(lo, z, qa),
                               jnp.where(lo, qc, z), jnp.where(lo, z, qc)], axis=0)
        kk, vv = keys_values(j, h)
        return lax.dot_general(lhs, kk, (((1,), (1,)), ((), ())), preferred_element_type=F32), vv

    def softmax(j, h, sc):
        if is_meta:
            madd = mask_ref[2]
        elif j == 0:
            madd = mask_ref[jnp.where(i == 0, 1, 0)]
        else:
            madd = mask_ref[0]
        ps = []
        for g in range(GROUP):
            sg = sc[g * BLOCK:(g + 1) * BLOCK] + madd
            sink = sinks_ref[attn_idx, h * GROUP + g]
            m = jnp.maximum(jnp.max(sg, axis=-1, keepdims=True), sink)
            e = jnp.exp(sg - m)
            den = jnp.sum(e, axis=-1, keepdims=True) + jnp.exp(sink - m)
            ps.append((e / den).astype(BF16))
        return jnp.concatenate(ps, axis=0)

    def weighted_values(j, h, p, vv):
        rows = block_rows(j)
        c0 = h * GROUP * HEAD_DIM
        res = _dot(p, vv)
        oa = jnp.where(lo, res[0:BLOCK], res[BLOCK:2 * BLOCK])
        oc = jnp.where(lo, res[2 * BLOCK:3 * BLOCK], res[3 * BLOCK:])
        ob_ref[rows, c0:c0 + LANES] = oa.astype(BF16)
        ob_ref[rows, c0 + LANES:c0 + 2 * LANES] = oc.astype(BF16)

    def attn_out_rows(r, h1_new_ref):
        rows = block_rows(r)
        mix = _dot(ob_ref[rows, :], wao_ref[...]) + bao_ref[...]
        first_norm(r, x_ref[0, rows, :], mix, h1_new_ref)

    gu_rows = min(GATE_ROWS, T)
    gu_ids = [(c, m) for c in range(N_FF_CHUNKS) for m in range(T // gu_rows)]

    def gate_up(c, m):
        hb = hb_ref[m * gu_rows:(m + 1) * gu_rows, :]
        return (_dot(hb, wi_ref[:, c * FF_CHUNK:(c + 1) * FF_CHUNK]),
                _dot(hb, wi_ref[:, D_FF + c * FF_CHUNK:D_FF + (c + 1) * FF_CHUNK]))

    def conv_rows(c, m, g, u):
        cols = slice(c * FF_CHUNK, (c + 1) * FF_CHUNK)
        if is_meta:
            g = jnp.where(jnp.concatenate([live_rows(r) for r in range(n_rb)], axis=0), g, 0.0)
        gext = jnp.concatenate([gcar_ref[:, cols], g], axis=0)
        g1 = pltpu.roll(gext, 1, 0)[SUBLANES:]
        g2 = pltpu.roll(gext, 2, 0)[SUBLANES:]
        gcar_ref[:, cols] = g[gu_rows - SUBLANES:]
        cc = (cb_ref[:, cols] + g2 * cw_ref[0:1, cols] + g1 * cw_ref[1:2, cols]
              + g * cw_ref[2:3, cols])
        hh_ref[m * gu_rows:(m + 1) * gu_rows, cols] = (_silu(cc) * u).astype(BF16)

    def body(h1_new_ref, h1_old_ref):
        f_pieces = {}
        downs = [functools.partial(down_piece, f_pieces, n, m) for n, m in down_ids]
        if kind == "pool":
            per = -(-len(downs) // n_rb)
            for r in range(n_rb):
                for _ in range(min(per, len(downs))):
                    downs.pop(0)()
                pool_rows(r, h1_new_ref)
            tail = x_ref[0, T - MAX_POOL:, :]
            if is_meta:
                tail = jnp.where(live_rows(n_rb - 1)[BLOCK - MAX_POOL:], tail, 0.0)
            halo_ref[...] = tail
        else:
            for r in range(n_rb):
                q_rows(r)
            units = [(j, h) for j in range(n_rb) for h in range(N_KV_HEADS)]
            per = -(-len(downs) // len(units))
            stride = max(len(units) // len(downs), 1)
            nxt = scores(*units[0])
            for p, (j, h) in enumerate(units):
                sc, vv = nxt
                if p + 1 < len(units):
                    nxt = scores(*units[p + 1])
                if p % stride == min(1, stride - 1):
                    for _ in range(min(per, len(downs))):
                        downs.pop(0)()
                weighted_values(j, h, softmax(j, h, sc), vv)
                if h == N_KV_HEADS - 1:
                    attn_out_rows(j, h1_new_ref)
        for job in downs:
            job()

        under = [functools.partial(finish_rows, r, f_pieces, h1_old_ref) for r in range(n_rb)]
        if emit_kv:
            under += [functools.partial(kv_block, n) for n in range(2 * KV_DUP // OUT_BLOCK)]
        nxt = gate_up(*gu_ids[0])
        for k, (c, m) in enumerate(gu_ids):
            g, u = nxt
            if k + 1 < len(gu_ids):
                nxt = gate_up(*gu_ids[k + 1])
            conv_rows(c, m, g, u)
            if under and (k + 1) % (T // gu_rows) == 0:
                under.pop(0)()
        for job in under:
            job()
        gtail_ref[0] = gcar_ref[...]

    @pl.when(jnp.logical_and(has_cur, s % 2 == 0))
    def _():
        body(h1_even_ref, h1_odd_ref)

    @pl.when(jnp.logical_and(has_cur, s % 2 == 1))
    def _():
        body(h1_odd_ref, h1_even_ref)

    @pl.when(s == n_steps - 1)
    def _():
        h1_old_ref = h1_even_ref if (n_steps - 2) % 2 == 0 else h1_odd_ref
        f_pieces = {}
        for n, m in down_ids:
            down_piece(f_pieces, n, m)
        for r in range(n_rb):
            finish_rows(r, f_pieces, h1_old_ref)
        if emit_kv:
            for n in range(2 * KV_DUP // OUT_BLOCK):
                kv_block(n)

    if emit_kv:
        @pl.when(jnp.logical_and(s > 0, (s - 1) % n_tiles == n_tiles - 1))
        def _():
            kvtail_ref[0] = _dot(h2b_ref[T - WINDOW:, :], wkv_ref[...]) + bkv_ref[...]


def _prompt_layer(x, *, layer, kind, is_meta, emit_kv, tile, name, halo_in=None, gcar_in,
                  pool=None, attn=None, ln, ffn, kv=None, cast_next=()):
    B, L, _ = x.shape
    n_tiles = L // tile
    n_all = B * n_tiles
    n_steps = n_all + 1
    nb = tile // BLOCK
    attn_idx = layer - N_A

    def cur(s):
        t = jnp.minimum(s, n_all - 1)
        return t // n_tiles, t % n_tiles

    def prev(s):
        t = jnp.maximum(s - 1, 0)
        return t // n_tiles, t % n_tiles

    cur_rows = lambda w: pl.BlockSpec((1, tile, w), lambda s: (*cur(s), 0))
    prev_rows = lambda w: pl.BlockSpec((1, tile, w), lambda s: (*prev(s), 0))
    ins, specs = [x], [cur_rows(D_MODEL)]

    def add(a, spec):
        ins.append(a)
        specs.append(spec)

    if kind == "pool":
        add(halo_in, _const_spec(halo_in))
    add(gcar_in, _const_spec(gcar_in))
    if kind == "pool":
        for a in pool:
            add(a, _layer_spec(a, layer))
    else:
        kd, vd, kmeta, vmeta, mask, sinks, wq, bq, wao, bao = attn

        def before(s):
            b, i = cur(s)
            return b, jnp.maximum(i * nb - 1, 0), 0

        before_spec = pl.BlockSpec((1, BLOCK, KV_DUP), before)
        ins += [kd, vd, kd, vd]
        specs += [cur_rows(KV_DUP), cur_rows(KV_DUP), before_spec, before_spec]
        for a in (kmeta, vmeta, mask):
            add(a, _const_spec(a))
        add(sinks, pl.BlockSpec(memory_space=pltpu.SMEM))
        for a in (wq, bq, wao, bao):
            add(a, _layer_spec(a, attn_idx))
    wi, cw, cb, wo = ffn
    for a in ln:
        add(a, _layer_spec(a, layer))
    add(wi, _const_spec(wi))
    add(cw, _layer_spec(cw, layer))
    add(cb, _layer_spec(cb, layer))
    add(wo, _const_spec(wo))
    if emit_kv:
        for a in kv:
            add(a, _const_spec(a))
    for a in cast_next:
        slab = a.shape[1] // CAST_SLABS
        add(a.reshape(a.shape[0], CAST_SLABS, slab, a.shape[2]),
            pl.BlockSpec((None, None, slab, a.shape[2]),
                         lambda s: (layer + 1, jnp.minimum(s, CAST_SLABS - 1), 0, 0)))

    out_shape = [jax.ShapeDtypeStruct((B, L, D_MODEL), F32),
                 jax.ShapeDtypeStruct((B, SUBLANES, D_FF), F32)]
    out_specs = [prev_rows(D_MODEL), pl.BlockSpec((1, SUBLANES, D_FF), lambda s: (cur(s)[0], 0, 0))]
    if emit_kv:
        out_shape += [jax.ShapeDtypeStruct((B, L, KV_DUP), BF16),
                      jax.ShapeDtypeStruct((B, L, KV_DUP), BF16),
                      jax.ShapeDtypeStruct((B, WINDOW, 2 * KV_DIM), F32)]
        out_specs += [prev_rows(KV_DUP), prev_rows(KV_DUP),
                      pl.BlockSpec((1, WINDOW, 2 * KV_DIM), lambda s: (prev(s)[0], 0, 0))]
    for a in cast_next:
        slab = a.shape[1] // CAST_SLABS
        out_shape.append(jax.ShapeDtypeStruct((CAST_SLABS, slab, a.shape[2]), BF16))
        out_specs.append(pl.BlockSpec((None, slab, a.shape[2]),
                                      lambda s: (jnp.minimum(s, CAST_SLABS - 1), 0, 0)))
    scratch = [pltpu.VMEM((SUBLANES, D_FF), F32), pltpu.VMEM((tile, D_FF), BF16),
               pltpu.VMEM((tile, D_MODEL), F32), pltpu.VMEM((tile, D_MODEL), F32),
               pltpu.VMEM((tile, D_MODEL), BF16)]
    if kind == "pool":
        scratch.append(pltpu.VMEM((MAX_POOL, D_MODEL), F32))
    else:
        scratch += [pltpu.VMEM((tile, D_MODEL), BF16), pltpu.VMEM((tile, D_MODEL), BF16)]
    if emit_kv:
        scratch.append(pltpu.VMEM((tile, D_MODEL), BF16))
    body = functools.partial(_prompt_kernel, kind=kind, is_meta=is_meta, emit_kv=emit_kv,
                             tile=tile, n_tiles=n_tiles, n_steps=n_steps, attn_idx=attn_idx,
                             n_cast=len(cast_next))
    return pl.pallas_call(
        body,
        grid=(n_steps,),
        in_specs=specs,
        out_specs=out_specs,
        out_shape=out_shape,
        scratch_shapes=scratch,
        compiler_params=pltpu.CompilerParams(
            dimension_semantics=("arbitrary",), vmem_limit_bytes=VMEM_LIMIT),
        name=name,
    )(*ins)


def _sample_kernel(*refs, kind, emit_kv, emit_q):
    it = iter(refs)
    x_ref = next(it)
    if kind == "pool":
        sp_ref, pw_ref, ps_ref = next(it), next(it), next(it)
    else:
        o_ref, wao_ref, bao_ref = next(it), next(it), next(it)
    lmg_ref, lmb_ref, lfg_ref, lfb_ref = (next(it) for _ in range(4))
    wg_ref, wu_ref, cw_ref, cb_ref, wo_ref, s0_ref, s1_ref = (next(it) for _ in range(7))
    if emit_kv:
        wkv_ref, bkv_ref = next(it), next(it)
    if emit_q:
        wq_ref, bq_ref = next(it), next(it)
    y_ref, g_ref = next(it), next(it)
    if emit_kv:
        kv_ref = next(it)
    if emit_q:
        q_ref = next(it)
    h1_ref, h1b_ref, acc_ref = next(it), next(it), next(it)

    c = pl.program_id(0)

    @pl.when(c == 0)
    def _():
        x = x_ref[...]
        if kind == "pool":
            parts = []
            for gi, w in enumerate(POOL_WINDOWS):
                cols = slice(gi * POOL_GC, (gi + 1) * POOL_GC)
                win = x[:, cols]
                for k in range(1, w):
                    win = win + sp_ref[POOL_STATE - k, :, cols]
                dg = win * (1.0 / w) - x[:, cols]
                parts.append(_dot(dg.astype(BF16), pw_ref[gi]))
            mix = jnp.concatenate(parts, axis=1) * ps_ref[...]
        else:
            mix = _dot(o_ref[...], wao_ref[...]) + bao_ref[...]
        h1 = _layer_norm(ALPHA * x + mix, lmg_ref[...], lmb_ref[...])
        h1_ref[...] = h1
        h1b_ref[...] = h1.astype(BF16)
        acc_ref[...] = jnp.zeros_like(acc_ref)

    h1b = h1b_ref[...]
    g = _dot(h1b, wg_ref[...])
    u = _dot(h1b, wu_ref[...])
    g_ref[...] = g
    cc = cb_ref[...] + s0_ref[...] * cw_ref[0:1, :] + s1_ref[...] * cw_ref[1:2, :] + g * cw_ref[2:3, :]
    acc_ref[...] += _dot((_silu(cc) * u).astype(BF16), wo_ref[...])

    @pl.when(c == N_DEC_CHUNKS - 1)
    def _():
        h2 = _layer_norm(ALPHA * h1_ref[...] + acc_ref[...], lfg_ref[...], lfb_ref[...])
        y_ref[...] = h2
        h2b = h2.astype(BF16)
        if emit_kv:
            kv_ref[...] = _dot(h2b, wkv_ref[...]) + bkv_ref[...]
        if emit_q:
            q_ref[...] = _dot(h2b, wq_ref[...]) + bq_ref[...]


def _sample_layer(x, *, layer, kind, emit_kv, emit_q, name, mixer, ln, ffn, conv_t, kv=None, qproj=None):
    n = x.shape[0]
    ins, specs = [x], [_const_spec(x)]

    def add(a, spec):
        ins.append(a)
        specs.append(spec)

    if kind == "pool":
        for a in mixer:
            add(a, _layer_spec(a, layer))
    else:
        o, wao, bao = mixer
        add(o, _const_spec(o))
        add(wao, _layer_spec(wao, layer - N_A))
        add(bao, _layer_spec(bao, layer - N_A))
    for a in ln:
        add(a, _layer_spec(a, layer))
    wi, cw, cb, wo = ffn
    add(wi, pl.BlockSpec((D_MODEL, DEC_CHUNK), lambda c: (0, c)))
    add(wi, pl.BlockSpec((D_MODEL, DEC_CHUNK), lambda c: (0, N_DEC_CHUNKS + c)))
    add(cw, pl.BlockSpec((None, CONV_W, DEC_CHUNK), lambda c: (layer, 0, c)))
    add(cb, pl.BlockSpec((None, 1, DEC_CHUNK), lambda c: (layer, 0, c)))
    add(wo, pl.BlockSpec((DEC_CHUNK, D_MODEL), lambda c: (c, 0)))
    add(conv_t, pl.BlockSpec((None, None, n, DEC_CHUNK), lambda c: (layer, 0, 0, c)))
    add(conv_t, pl.BlockSpec((None, None, n, DEC_CHUNK), lambda c: (layer, 1, 0, c)))
    if emit_kv:
        for a in kv:
            add(a, _const_spec(a))
    if emit_q:
        for a in qproj:
            add(a, _layer_spec(a, layer + 1 - N_A))

    full = lambda w: pl.BlockSpec((n, w), lambda c: (0, 0))
    out_shape = [jax.ShapeDtypeStruct((n, D_MODEL), F32), jax.ShapeDtypeStruct((n, D_FF), F32)]
    out_specs = [full(D_MODEL), pl.BlockSpec((n, DEC_CHUNK), lambda c: (0, c))]
    if emit_kv:
        out_shape.append(jax.ShapeDtypeStruct((n, 2 * KV_DIM), F32))
        out_specs.append(full(2 * KV_DIM))
    if emit_q:
        out_shape.append(jax.ShapeDtypeStruct((n, D_MODEL), F32))
        out_specs.append(full(D_MODEL))
    body = functools.partial(_sample_kernel, kind=kind, emit_kv=emit_kv, emit_q=emit_q)
    return pl.pallas_call(
        body,
        grid=(N_DEC_CHUNKS,),
        in_specs=specs,
        out_specs=out_specs,
        out_shape=out_shape,
        scratch_shapes=[pltpu.VMEM((n, D_MODEL), F32), pltpu.VMEM((n, D_MODEL), BF16),
                        pltpu.VMEM((n, D_MODEL), F32)],
        compiler_params=pltpu.CompilerParams(dimension_semantics=("arbitrary",),
                                             vmem_limit_bytes=VMEM_LIMIT),
        name=name,
    )(*ins)


def _sample_attn_kernel(*refs, update):
    it = iter(refs)
    q_ref, kt_ref, vt_ref = next(it), next(it), next(it)
    if update:
        kn_ref, vn_ref = next(it), next(it)
    sink_ref, o_ref = next(it), next(it)
    if update:
        nkt_ref, nvt_ref = next(it), next(it)

    kt, vt = kt_ref[...], vt_ref[...]
    if update:
        newest = lax.broadcasted_iota(jnp.int32, (KV_DIM, WINDOW), 1) == WINDOW - 1

        def advance(win, new_ref):
            return jnp.stack([jnp.where(newest, new_ref[:, b:b + 1], pltpu.roll(win[b], WINDOW - 1, 1))
                              for b in range(SAMPLE_GROUP)])

        kt, vt = advance(kt, kn_ref), advance(vt, vn_ref)
        nkt_ref[...] = kt
        nvt_ref[...] = vt
    q = q_ref[...]
    s = jnp.einsum("bhc,bcj->bhj", q, kt.astype(BF16), preferred_element_type=F32) * ATTN_SCALE
    sink = sink_ref[...][None]
    m = jnp.max(jnp.maximum(s, sink), axis=-1, keepdims=True)
    e = jnp.exp(s - m)
    den = jnp.sum(e, axis=-1, keepdims=True) + jnp.exp(sink - m)
    p = (e / den).astype(BF16)
    o_ref[...] = jnp.einsum("bhj,bcj->bhc", p, vt.astype(BF16), preferred_element_type=F32).astype(BF16)


def _sample_attn(qexp, kt, vt, sinkb, layer, name, new_cols=None):
    n = qexp.shape[0]
    G = SAMPLE_GROUP
    update = new_cols is not None
    win_spec = pl.BlockSpec((G, KV_DIM, WINDOW), lambda i: (i, 0, 0))
    q_spec = pl.BlockSpec((G, N_HEADS, KV_DIM), lambda i: (i, 0, 0))
    ins, specs = [qexp, kt, vt], [q_spec, win_spec, win_spec]
    if update:
        col_spec = pl.BlockSpec((None, KV_DIM, G), lambda i: (i, 0, 0))
        ins += list(new_cols)
        specs += [col_spec, col_spec]
    ins.append(sinkb)
    specs.append(pl.BlockSpec((None, N_HEADS, WINDOW), lambda i: (layer - N_A, 0, 0)))
    out_shape = [jax.ShapeDtypeStruct((n, N_HEADS, KV_DIM), BF16)]
    out_specs = [q_spec]
    if update:
        out_shape += [jax.ShapeDtypeStruct(kt.shape, F32)] * 2
        out_specs += [win_spec, win_spec]
    return pl.pallas_call(
        functools.partial(_sample_attn_kernel, update=update),
        grid=(n // G,),
        in_specs=specs,
        out_specs=out_specs,
        out_shape=out_shape,
        compiler_params=pltpu.CompilerParams(dimension_semantics=("arbitrary",)),
        name=name,
    )(*ins)


def _dup_heads(w):
    lead = w.shape[:-1]
    w = w.reshape(*lead, N_KV_HEADS, 1, HEAD_DIM)
    return jnp.broadcast_to(w, (*lead, N_KV_HEADS, 2, HEAD_DIM)).reshape(*lead, KV_DUP)


def _expand_q(q):
    n = q.shape[0]
    q = q.reshape(n, N_KV_HEADS, GROUP, HEAD_DIM)
    parts = [jnp.pad(q[:, h], ((0, 0), (0, 0), (h * HEAD_DIM, KV_DIM - (h + 1) * HEAD_DIM)))
             for h in range(N_KV_HEADS)]
    return jnp.stack(parts, axis=1).reshape(n, N_HEADS, KV_DIM)


def _contract_o(res):
    n = res.shape[0]
    res = res.reshape(n, N_KV_HEADS, GROUP, N_KV_HEADS, HEAD_DIM)
    return jnp.stack([res[:, h, :, h, :] for h in range(N_KV_HEADS)], axis=1).reshape(n, N_HEADS * HEAD_DIM)


def _band_masks():
    i = jnp.arange(BLOCK)[:, None]
    j = jnp.arange(2 * BLOCK)[None, :]
    band = (j > i) & (j <= i + BLOCK)
    masks = jnp.stack([band, band & (j >= PAD_ROWS), band & (j >= BLOCK + PAD_ROWS)])
    return jnp.where(masks, 0.0, NEG).astype(F32)


def kernel(x_prompt, x_sample, state_pool, state_conv, state_k_win, state_v_win, meta_tokens, pool_w, pool_scale, w_kv, b_kv, attn_w_q, attn_b_q, attn_sinks, attn_w_o, attn_b_o, ffn_w_in, ffn_conv_w, ffn_conv_b, ffn_w_out, ln_mix_g, ln_mix_b, ln_ffn_g, ln_ffn_b):
    B, S, _ = x_prompt.shape
    n_dec = x_sample.shape[0]
    assert x_sample.shape[1] == 1 and S % ROW_TILE == 0 and n_dec % SAMPLE_GROUP == 0

    row = lambda a: a[:, None, :]
    wi_b, wo_b = ffn_w_in[0].astype(BF16), ffn_w_out[0].astype(BF16)
    conv_b = row(ffn_conv_b)
    ln =(row(ln_mix_g), row(ln_mix_b), row(ln_ffn_g), row(ln_ffn_b))
    pool = (pool_w.astype(BF16), row(pool_scale))
    wkv_b, bkv = w_kv.astype(BF16), b_kv[None, :]
    wkvd_b = jnp.concatenate([_dup_heads(w_kv[:, :KV_DIM]), _dup_heads(w_kv[:, KV_DIM:])], axis=1).astype(BF16)
    bkvd = jnp.concatenate([_dup_heads(b_kv[:KV_DIM]), _dup_heads(b_kv[KV_DIM:])])[None, :]
    kvw = (wkvd_b, bkvd, wkv_b, bkv)
    wq_b, bq = attn_w_q.astype(BF16), row(attn_b_q)
    wao_b, bao = attn_w_o.astype(BF16), row(attn_b_o)
    sinkb = jnp.broadcast_to(attn_sinks[:, :, None], (attn_sinks.shape[0], N_HEADS, WINDOW))
    masks = _band_masks()

    zeros_halo = jnp.zeros((MAX_POOL, D_MODEL), F32)
    zeros_gcar = jnp.zeros((SUBLANES, D_FF), F32)
    zeros_kv = jnp.zeros((BLOCK, KV_DUP), BF16)

    hm = jnp.concatenate([jnp.zeros((PAD_ROWS, D_MODEL), F32), meta_tokens.astype(F32)])[None]
    hp = x_prompt
    hs = x_sample.reshape(n_dec, D_MODEL)
    pool_t = jnp.transpose(state_pool, (0, 2, 1, 3))
    conv_t = jnp.transpose(state_conv, (0, 2, 1, 3))
    kt = jnp.transpose(state_k_win, (0, 2, 3, 1)).reshape(n_dec, KV_DIM, WINDOW)
    vt = jnp.transpose(state_v_win, (0, 2, 3, 1)).reshape(n_dec, KV_DIM, WINDOW)

    new_pool_p, new_pool_t, new_conv_p, new_conv_t = [], [], [], []
    for l in range(DEPTH):
        ffn = (wi_b, ffn_conv_w, conv_b, wo_b)
        cast_next = (ffn_w_in, ffn_w_out) if l + 1 < DEPTH else ()
        if l < N_A:
            emit_kv = l == N_A - 1
            new_pool_p.append(hp[:, S - POOL_STATE:])
            new_pool_t.append(jnp.concatenate([pool_t[l, 1:], hs[None]], axis=0))
            outs_m = _prompt_layer(hm, layer=l, kind="pool", is_meta=True, emit_kv=emit_kv, tile=BLOCK,
                                   name=f"meta_l{l}", halo_in=zeros_halo, gcar_in=zeros_gcar,
                                   pool=pool, ln=ln, ffn=ffn, kv=kvw)
            outs_p = _prompt_layer(hp, layer=l, kind="pool", is_meta=False, emit_kv=emit_kv, tile=ROW_TILE,
                                   name=f"prompt_l{l}", halo_in=hm[0, PAD_ROWS:], gcar_in=outs_m[1][0],
                                   pool=pool, ln=ln, ffn=ffn, kv=kvw, cast_next=cast_next)
            outs_s = _sample_layer(hs, layer=l, kind="pool", emit_kv=emit_kv, emit_q=emit_kv,
                                   name=f"sample_l{l}", mixer=(pool_t, *pool), ln=ln, ffn=ffn, conv_t=conv_t,
                                   kv=(wkv_b, bkv), qproj=(wq_b, bq))
            if emit_kv:
                kd_m, vd_m = outs_m[2][0], outs_m[3][0]
                kd_p, vd_p, kvtail = outs_p[2], outs_p[3], outs_p[4]
                kv_s, q_s = outs_s[2], outs_s[3]
        else:
            attn_w = (masks, attn_sinks, wq_b, bq, wao_b, bao)
            outs_m = _prompt_layer(hm, layer=l, kind="attn", is_meta=True, emit_kv=False, tile=BLOCK,
                                   name=f"meta_l{l}", gcar_in=zeros_gcar,
                                   attn=(kd_m[None], vd_m[None], zeros_kv, zeros_kv, *attn_w), ln=ln, ffn=ffn)
            outs_p = _prompt_layer(hp, layer=l, kind="attn", is_meta=False, emit_kv=False, tile=ROW_TILE,
                                   name=f"prompt_l{l}", gcar_in=outs_m[1][0],
                                   attn=(kd_p, vd_p, kd_m, vd_m, *attn_w), ln=ln, ffn=ffn,
                                   cast_next=cast_next)
            qexp = _expand_q(q_s.astype(BF16))
            if l == N_A:
                cols = lambda a: jnp.transpose(a.reshape(n_dec // SAMPLE_GROUP, SAMPLE_GROUP, KV_DIM), (0, 2, 1))
                res, kt, vt = _sample_attn(qexp, kt, vt, sinkb, l, f"sample_attn_l{l}",
                                           new_cols=(cols(kv_s[:, :KV_DIM]), cols(kv_s[:, KV_DIM:])))
            else:
                res = _sample_attn(qexp, kt, vt, sinkb, l, f"sample_attn_l{l}")[0]
            emit_q = l + 1 < DEPTH
            outs_s = _sample_layer(hs, layer=l, kind="attn", emit_kv=False, emit_q=emit_q, name=f"sample_l{l}",
                                   mixer=(_contract_o(res), wao_b, bao), ln=ln, ffn=ffn, conv_t=conv_t,
                                   qproj=(wq_b, bq))
            if emit_q:
                q_s = outs_s[2]
        new_conv_p.append(outs_p[1][:, SUBLANES - (CONV_W - 1):])
        new_conv_t.append(jnp.stack([conv_t[l, 1], outs_s[1]]))
        hm, hp, hs = outs_m[0], outs_p[0], outs_s[0]
        if cast_next:
            wi_b = outs_p[-2].reshape(D_MODEL, 2 * D_FF)
            wo_b = outs_p[-1].reshape(D_FF, D_MODEL)

    unwin = lambda t: jnp.transpose(t.reshape(-1, N_KV_HEADS, HEAD_DIM, WINDOW), (0, 3, 1, 2))
    new_k_p = kvtail[:, :, :KV_DIM].reshape(B, WINDOW, N_KV_HEADS, HEAD_DIM)
    new_v_p = kvtail[:, :, KV_DIM:].reshape(B, WINDOW, N_KV_HEADS, HEAD_DIM)
    return (hp, hs.reshape(n_dec, 1, D_MODEL),
            jnp.stack(new_pool_p), jnp.transpose(jnp.stack(new_pool_t), (0, 2, 1, 3)),
            jnp.stack(new_conv_p), jnp.transpose(jnp.stack(new_conv_t), (0, 2, 1, 3)),
            new_k_p, new_v_p, unwin(kt), unwin(vt))
```

```python
import functools

import jax
import jax.numpy as jnp
from jax import lax
from jax.experimental import pallas as pl
from jax.experimental.pallas import tpu as pltpu

F32 = jnp.float32
BF16 = jnp.bfloat16

D_MODEL = 1024
DEPTH = 4
N_META = 16
N_A = DEPTH // 2
POOL_WINDOWS = (2, 4, 8, 16)
N_POOL_GROUPS = len(POOL_WINDOWS)
POOL_GC = D_MODEL // N_POOL_GROUPS
MAX_POOL = max(POOL_WINDOWS)
POOL_STATE = MAX_POOL - 1
HEAD_DIM = 64
N_HEADS = D_MODEL // HEAD_DIM
N_KV_HEADS = 4
GROUP = N_HEADS // N_KV_HEADS
KV_DIM = N_KV_HEADS * HEAD_DIM
WINDOW = 128
BLOCK = 128
ATTN_SCALE = HEAD_DIM ** -0.5
D_FF = 2816
CONV_W = 3
ALPHA = (2.0 * DEPTH) ** 0.25
LN_EPS = 1e-5
NEG = -1e30

SUBLANES = 8
LANES = 128
PAD_ROWS = BLOCK - N_META
FF_CHUNK = 256
N_FF_CHUNKS = D_FF // FF_CHUNK
OUT_BLOCK = 256
N_OUT_BLOCKS = D_MODEL // OUT_BLOCK
DEC_CHUNK = D_FF // 2
N_DEC_CHUNKS = D_FF // DEC_CHUNK
CAST_SLABS = 16
DOWN_ROWS = 256
GATE_ROWS = 128
ROW_TILE = 512
SAMPLE_GROUP = 16
DUP = 2 * HEAD_DIM
KV_DUP = N_KV_HEADS * DUP
VMEM_LIMIT = 58 * 1024 * 1024


def _layer_norm(z, g, b):
    mu = jnp.mean(z, axis=-1, keepdims=True)
    zc = z - mu
    var = jnp.mean(zc * zc, axis=-1, keepdims=True)
    return zc * lax.rsqrt(var + LN_EPS) * g + b


def _silu(c):
    return c * (1.0 / (1.0 + jnp.exp(-c)))


def _dot(a, b):
    return jnp.dot(a, b, preferred_element_type=F32)


def _const_spec(a):
    nd = a.ndim
    return pl.BlockSpec(a.shape, lambda *_: (0,) * nd, pipeline_mode=pl.Buffered(1))


def _layer_spec(a, l):
    nd = a.ndim
    return pl.BlockSpec((None,) + a.shape[1:], lambda *_: (l,) + (0,) * (nd - 1),
                        pipeline_mode=pl.Buffered(1))


def _prompt_kernel(*refs, kind, is_meta, emit_kv, tile, n_tiles, n_steps, attn_idx, n_cast):
    it = iter(refs)
    x_ref = next(it)
    halo_in_ref = next(it) if kind == "pool" else None
    gcar_in_ref = next(it)
    if kind == "pool":
        pw_ref, ps_ref = next(it), next(it)
    else:
        kcur_ref, vcur_ref, kprev_ref, vprev_ref, kmeta_ref, vmeta_ref = (next(it) for _ in range(6))
        mask_ref, sinks_ref = next(it), next(it)
        wq_ref, bq_ref, wao_ref, bao_ref = (next(it) for _ in range(4))
    lmg_ref, lmb_ref, lfg_ref, lfb_ref = (next(it) for _ in range(4))
    wi_ref, cw_ref, cb_ref, wo_ref = (next(it) for _ in range(4))
    if emit_kv:
        wkvd_ref, bkvd_ref, wkv_ref, bkv_ref = (next(it) for _ in range(4))
    cast_in_refs = [next(it) for _ in range(n_cast)]
    y_ref, gtail_ref = next(it), next(it)
    if emit_kv:
        kd_ref, vd_ref, kvtail_ref = next(it), next(it), next(it)
    for src_ref in cast_in_refs:
        next(it)[...] = src_ref[...].astype(BF16)
    gcar_ref, hh_ref, h1_ref, hb_ref = (next(it) for _ in range(4))
    if kind == "pool":
        halo_ref = next(it)
    else:
        qb_ref, ob_ref = next(it), next(it)
    if emit_kv:
        h2b_ref = next(it)

    T = tile
    n_rb = T // BLOCK
    s = pl.program_id(0)
    i = s % n_tiles
    has_cur = s < n_steps - 1

    @pl.when(s == 0)
    def _():
        hh_ref[...] = jnp.zeros_like(hh_ref)
        h1_ref[...] = jnp.zeros_like(h1_ref)

    @pl.when(jnp.logical_and(has_cur, i == 0))
    def _():
        gcar_ref[...] = gcar_in_ref[...]
        if kind == "pool":
            halo_ref[...] = halo_in_ref[...]

    def block_rows(r):
        return slice(r * BLOCK, (r + 1) * BLOCK)

    def live_rows(r):
        return lax.broadcasted_iota(jnp.int32, (BLOCK, 1), 0) + r * BLOCK >= PAD_ROWS

    down_rows = min(DOWN_ROWS, T)
    down_ids = [(n, m) for m in range(T // down_rows) for n in range(N_OUT_BLOCKS)]

    def down_piece(f_pieces, n, m):
        f_pieces[n, m] = _dot(hh_ref[m * down_rows:(m + 1) * down_rows, :],
                              wo_ref[:, n * OUT_BLOCK:(n + 1) * OUT_BLOCK])

    def finish_rows(r, f_pieces, h1_old_ref):
        rows = block_rows(r)
        m, off = divmod(r * BLOCK, down_rows)
        f = jnp.concatenate([f_pieces[n, m][off:off + BLOCK] for n in range(N_OUT_BLOCKS)], axis=1)
        h2 = _layer_norm(ALPHA * h1_old_ref[rows, :] + f, lfg_ref[...], lfb_ref[...])
        y_ref[0, rows, :] = h2
        if emit_kv:
            h2b_ref[rows, :] = h2.astype(BF16)

    def kv_block(n):
        cols = slice(n * OUT_BLOCK, (n + 1) * OUT_BLOCK)
        kvd = (_dot(h2b_ref[...], wkvd_ref[:, cols]) + bkvd_ref[:, cols]).astype(BF16)
        if n * OUT_BLOCK < KV_DUP:
            kd_ref[0, :, cols] = kvd
        else:
            vd_ref[0, :, n * OUT_BLOCK - KV_DUP:(n + 1) * OUT_BLOCK - KV_DUP] = kvd

    def first_norm(r, x_blk, mix, h1_new_ref):
        rows = block_rows(r)
        h1 = _layer_norm(ALPHA * x_blk + mix, lmg_ref[...], lmb_ref[...])
        h1_new_ref[rows, :] = h1
        hb_ref[rows, :] = h1.astype(BF16)

    def pool_rows(r, h1_new_ref):
        rows = block_rows(r)
        x_blk = x_ref[0, rows, :]
        if is_meta:
            x_blk = jnp.where(live_rows(r), x_blk, 0.0)
        if r == 0:
            xe = jnp.concatenate([halo_ref[...], x_blk], axis=0)
        else:
            xe = x_ref[0, r * BLOCK - MAX_POOL:(r + 1) * BLOCK, :]
        s2 = xe + pltpu.roll(xe, 1, 0)
        s2r = s2[:, POOL_GC:]
        s4 = s2r + pltpu.roll(s2r, 2, 0)
        s4r = s4[:, POOL_GC:]
        s8 = s4r + pltpu.roll(s4r, 4, 0)
        s8r = s8[:, POOL_GC:]
        s16 = s8r + pltpu.roll(s8r, 8, 0)
        wins = (s2[MAX_POOL:, :POOL_GC], s4[MAX_POOL:, :POOL_GC],
                s8[MAX_POOL:, :POOL_GC], s16[MAX_POOL:])
        parts = []
        for gi, w in enumerate(POOL_WINDOWS):
            xg = x_blk[:, gi * POOL_GC:(gi + 1) * POOL_GC]
            if is_meta:
                t1 = lax.broadcasted_iota(jnp.int32, (BLOCK, 1), 0) + (r * BLOCK - PAD_ROWS + 1)
                dg = wins[gi] / jnp.clip(t1, 1, w).astype(F32) - xg
            else:
                dg = wins[gi] * (1.0 / w) - xg
            parts.append(_dot(dg.astype(BF16), pw_ref[gi]))
        mix = jnp.concatenate(parts, axis=1) * ps_ref[...]
        first_norm(r, x_blk, mix, h1_new_ref)

    def q_rows(r):
        rows = block_rows(r)
        q = (_dot(x_ref[0, rows, :].astype(BF16), wq_ref[...]) + bq_ref[...]) * ATTN_SCALE
        qb_ref[rows, :] = q.astype(BF16)

    lo = lax.broadcasted_iota(jnp.int32, (BLOCK, LANES), 1) < HEAD_DIM

    def keys_values(j, h):
        cols = slice(h * DUP, (h + 1) * DUP)
        if j == 0:
            first = i == 0
            kp = jnp.where(first, kmeta_ref[:, cols], kprev_ref[0, :, cols])
            vp = jnp.where(first, vmeta_ref[:, cols], vprev_ref[0, :, cols])
            return (jnp.concatenate([kp, kcur_ref[0, 0:BLOCK, cols]], axis=0),
                    jnp.concatenate([vp, vcur_ref[0, 0:BLOCK, cols]], axis=0))
        return (kcur_ref[0, (j - 1) * BLOCK:(j + 1) * BLOCK, cols],
                vcur_ref[0, (j - 1) * BLOCK:(j + 1) * BLOCK, cols])

    def scores(j, h):
        rows = block_rows(j)
        c0 = h * GROUP * HEAD_DIM
        qa = qb_ref[rows, c0:c0 + LANES]
        qc = qb_ref[rows, c0 + LANES:c0 + 2 * LANES]
        z = jnp.zeros_like(qa)
        lhs = jnp.concatenate([jnp.where(lo, qa, z), jnp.where(lo, z, qa),
                               jnp.where(lo, qc, z), jnp.where(lo, z, qc)], axis=0)
        kk, vv = keys_values(j, h)
        return lax.dot_general(lhs, kk, (((1,), (1,)), ((), ())), preferred_element_type=F32), vv

    def softmax(j, h, sc):
        if is_meta:
            madd = mask_ref[2]
        elif j == 0:
            madd = mask_ref[jnp.where(i == 0, 1, 0)]
        else:
            madd = mask_ref[0]
        ps = []
        for g in range(GROUP):
            sg = sc[g * BLOCK:(g + 1) * BLOCK] + madd
            sink = sinks_ref[attn_idx, h * GROUP + g]
            m = jnp.maximum(jnp.max(sg, axis=-1, keepdims=True), sink)
            e = jnp.exp(sg - m)
            den = jnp.sum(e, axis=-1, keepdims=True) + jnp.exp(sink - m)
            ps.append((e / den).astype(BF16))
        return jnp.concatenate(ps, axis=0)

    def weighted_values(j, h, p, vv):
        rows = block_rows(j)
        c0 = h * GROUP * HEAD_DIM
        res = _dot(p, vv)
        oa = jnp.where(lo, res[0:BLOCK], res[BLOCK:2 * BLOCK])
        oc = jnp.where(lo, res[2 * BLOCK:3 * BLOCK], res[3 * BLOCK:])
        ob_ref[rows, c0:c0 + LANES] = oa.astype(BF16)
        ob_ref[rows, c0 + LANES:c0 + 2 * LANES] = oc.astype(BF16)

    def attn_out_rows(r, h1_new_ref):
        rows = block_rows(r)
        mix = _dot(ob_ref[rows, :], wao_ref[...]) + bao_ref[...]
        first_norm(r, x_ref[0, rows, :], mix, h1_new_ref)

    gu_rows = min(GATE_ROWS, T)
    gu_ids = [(c, m) for c in range(N_FF_CHUNKS) for m in range(T // gu_rows)]

    def gate_up(c, m):
        hb = hb_ref[m * gu_rows:(m + 1) * gu_rows, :]
        return (_dot(hb, wi_ref[:, c * FF_CHUNK:(c + 1) * FF_CHUNK]),
                _dot(hb, wi_ref[:, D_FF + c * FF_CHUNK:D_FF + (c + 1) * FF_CHUNK]))

    def conv_rows(c, m, g, u):
        cols = slice(c * FF_CHUNK, (c + 1) * FF_CHUNK)
        if is_meta:
            g = jnp.where(jnp.concatenate([live_rows(r) for r in range(n_rb)], axis=0), g, 0.0)
        gext = jnp.concatenate([gcar_ref[:, cols], g], axis=0)
        g1 = pltpu.roll(gext, 1, 0)[SUBLANES:]
        g2 = pltpu.roll(gext, 2, 0)[SUBLANES:]
        gcar_ref[:, cols] = g[gu_rows - SUBLANES:]
        cc = (cb_ref[:, cols] + g2 * cw_ref[0:1, cols] + g1 * cw_ref[1:2, cols]
              + g * cw_ref[2:3, cols])
        hh_ref[m * gu_rows:(m + 1) * gu_rows, cols] = (_silu(cc) * u).astype(BF16)

    def body(h1_new_ref, h1_old_ref):
        f_pieces = {}
        downs = [functools.partial(down_piece, f_pieces, n, m) for n, m in down_ids]
        if kind == "pool":
            per = -(-len(downs) // n_rb)
            for r in range(n_rb):
                for _ in range(min(per, len(downs))):
                    downs.pop(0)()
                pool_rows(r, h1_new_ref)
            tail = x_ref[0, T - MAX_POOL:, :]
            if is_meta:
                tail = jnp.where(live_rows(n_rb - 1)[BLOCK - MAX_POOL:], tail, 0.0)
            halo_ref[...] = tail
        else:
            for r in range(n_rb):
                q_rows(r)
            units = [(j, h) for j in range(n_rb) for h in range(N_KV_HEADS)]
            per = -(-len(downs) // len(units))
            stride = max(len(units) // len(downs), 1)
            nxt = scores(*units[0])
            for p, (j, h) in enumerate(units):
                sc, vv = nxt
                if p + 1 < len(units):
                    nxt = scores(*units[p + 1])
                if p % stride == min(1, stride - 1):
                    for _ in range(min(per, len(downs))):
                        downs.pop(0)()
                weighted_values(j, h, softmax(j, h, sc), vv)
                if h == N_KV_HEADS - 1:
                    attn_out_rows(j, h1_new_ref)
        for job in downs:
            job()

        under = [functools.partial(finish_rows, r, f_pieces, h1_old_ref) for r in range(n_rb)]
        if emit_kv:
            under += [functools.partial(kv_block, n) for n in range(2 * KV_DUP // OUT_BLOCK)]
        nxt = gate_up(*gu_ids[0])
        for k, (c, m) in enumerate(gu_ids):
            g, u = nxt
            if k + 1 < len(gu_ids):
                nxt = gate_up(*gu_ids[k + 1])
            conv_rows(c, m, g, u)
            if under and (k + 1) % (T // gu_rows) == 0:
                under.pop(0)()
        for job in under:
            job()
        gtail_ref[0] = gcar_ref[...]

    @pl.when(has_cur)
    def _():
        body(h1_ref.at[s % 2], h1_ref.at[1 - s % 2])

    @pl.when(s == n_steps - 1)
    def _():
        h1_old_ref = h1_ref.at[(n_steps - 2) % 2]
        f_pieces = {}
        for n, m in down_ids:
            down_piece(f_pieces, n, m)
        for r in range(n_rb):
            finish_rows(r, f_pieces, h1_old_ref)
        if emit_kv:
            for n in range(2 * KV_DUP // OUT_BLOCK):
                kv_block(n)

    if emit_kv:
        @pl.when(jnp.logical_and(s > 0, (s - 1) % n_tiles == n_tiles - 1))
        def _():
            kvtail_ref[0] = _dot(h2b_ref[T - WINDOW:, :], wkv_ref[...]) + bkv_ref[...]


def _prompt_layer(x, *, layer, kind, is_meta, emit_kv, tile, name, halo_in=None, gcar_in,
                  pool=None, attn=None, ln, ffn, kv=None, cast_next=()):
    B, L, _ = x.shape
    n_tiles = L // tile
    n_all = B * n_tiles
    n_steps = n_all + 1
    nb = tile // BLOCK
    attn_idx = layer - N_A

    def cur(s):
        t = jnp.minimum(s, n_all - 1)
        return t // n_tiles, t % n_tiles

    def prev(s):
        t = jnp.maximum(s - 1, 0)
        return t // n_tiles, t % n_tiles

    cur_rows = lambda w: pl.BlockSpec((1, tile, w), lambda s: (*cur(s), 0))
    prev_rows = lambda w: pl.BlockSpec((1, tile, w), lambda s: (*prev(s), 0))
    ins, specs = [x], [cur_rows(D_MODEL)]

    def add(a, spec):
        ins.append(a)
        specs.append(spec)

    if kind == "pool":
        add(halo_in, _const_spec(halo_in))
    add(gcar_in, _const_spec(gcar_in))
    if kind == "pool":
        for a in pool:
            add(a, _layer_spec(a, layer))
    else:
        kd, vd, kmeta, vmeta, mask, sinks, wq, bq, wao, bao = attn

        def before(s):
            b, i = cur(s)
            return b, jnp.maximum(i * nb - 1, 0), 0

        before_spec = pl.BlockSpec((1, BLOCK, KV_DUP), before)
        ins += [kd, vd, kd, vd]
        specs += [cur_rows(KV_DUP), cur_rows(KV_DUP), before_spec, before_spec]
        for a in (kmeta, vmeta, mask):
            add(a, _const_spec(a))
        add(sinks, pl.BlockSpec(memory_space=pltpu.SMEM))
        for a in (wq, bq, wao, bao):
            add(a, _layer_spec(a, attn_idx))
    wi, cw, cb, wo = ffn
    for a in ln:
        add(a, _layer_spec(a, layer))
    add(wi, _const_spec(wi))
    add(cw, _layer_spec(cw, layer))
    add(cb, _layer_spec(cb, layer))
    add(wo, _const_spec(wo))
    if emit_kv:
        for a in kv:
            add(a, _const_spec(a))
    for a in cast_next:
        slab = a.shape[1] // CAST_SLABS
        add(a.reshape(a.shape[0], CAST_SLABS, slab, a.shape[2]),
            pl.BlockSpec((None, None, slab, a.shape[2]),
                         lambda s: (layer + 1, jnp.minimum(s, CAST_SLABS - 1), 0, 0)))

    out_shape = [jax.ShapeDtypeStruct((B, L, D_MODEL), F32),
                 jax.ShapeDtypeStruct((B, SUBLANES, D_FF), F32)]
    out_specs = [prev_rows(D_MODEL), pl.BlockSpec((1, SUBLANES, D_FF), lambda s: (cur(s)[0], 0, 0))]
    if emit_kv:
        out_shape += [jax.ShapeDtypeStruct((B, L, KV_DUP), BF16),
                      jax.ShapeDtypeStruct((B, L, KV_DUP), BF16),
                      jax.ShapeDtypeStruct((B, WINDOW, 2 * KV_DIM), F32)]
        out_specs += [prev_rows(KV_DUP), prev_rows(KV_DUP),
                      pl.BlockSpec((1, WINDOW, 2 * KV_DIM), lambda s: (prev(s)[0], 0, 0))]
    for a in cast_next:
        slab = a.shape[1] // CAST_SLABS
        out_shape.append(jax.ShapeDtypeStruct((CAST_SLABS, slab, a.shape[2]), BF16))
        out_specs.append(pl.BlockSpec((None, slab, a.shape[2]),
                                      lambda s: (jnp.minimum(s, CAST_SLABS - 1), 0, 0)))
    scratch = [pltpu.VMEM((SUBLANES, D_FF), F32), pltpu.VMEM((tile, D_FF), BF16),
               pltpu.VMEM((2, tile, D_MODEL), F32), pltpu.VMEM((tile, D_MODEL), BF16)]
    if kind == "pool":
        scratch.append(pltpu.VMEM((MAX_POOL, D_MODEL), F32))
    else:
        scratch += [pltpu.VMEM((tile, D_MODEL), BF16), pltpu.VMEM((tile, D_MODEL), BF16)]
    if emit_kv:
        scratch.append(pltpu.VMEM((tile, D_MODEL), BF16))
    body = functools.partial(_prompt_kernel, kind=kind, is_meta=is_meta, emit_kv=emit_kv,
                             tile=tile, n_tiles=n_tiles, n_steps=n_steps, attn_idx=attn_idx,
                             n_cast=len(cast_next))
    return pl.pallas_call(
        body,
        grid=(n_steps,),
        in_specs=specs,
        out_specs=out_specs,
        out_shape=out_shape,
        scratch_shapes=scratch,
        compiler_params=pltpu.CompilerParams(
            dimension_semantics=("arbitrary",), vmem_limit_bytes=VMEM_LIMIT),
        name=name,
    )(*ins)


def _sample_kernel(*refs, kind, emit_kv, emit_q):
    it = iter(refs)
    x_ref = next(it)
    if kind == "pool":
        sp_ref, pw_ref, ps_ref = next(it), next(it), next(it)
    else:
        o_ref, wao_ref, bao_ref = next(it), next(it), next(it)
    lmg_ref, lmb_ref, lfg_ref, lfb_ref = (next(it) for _ in range(4))
    wg_ref, wu_ref, cw_ref, cb_ref, wo_ref, s0_ref, s1_ref = (next(it) for _ in range(7))
    if emit_kv:
        wkv_ref, bkv_ref = next(it), next(it)
    if emit_q:
        wq_ref, bq_ref = next(it), next(it)
    y_ref, g_ref = next(it), next(it)
    if emit_kv:
        kv_ref = next(it)
    if emit_q:
        q_ref = next(it)
    h1_ref, h1b_ref, acc_ref = next(it), next(it), next(it)

    c = pl.program_id(0)

    @pl.when(c == 0)
    def _():
        x = x_ref[...]
        if kind == "pool":
            parts = []
            for gi, w in enumerate(POOL_WINDOWS):
                cols = slice(gi * POOL_GC, (gi + 1) * POOL_GC)
                win = x[:, cols]
                for k in range(1, w):
                    win = win + sp_ref[POOL_STATE - k, :, cols]
                dg = win * (1.0 / w) - x[:, cols]
                parts.append(_dot(dg.astype(BF16), pw_ref[gi]))
            mix = jnp.concatenate(parts, axis=1) * ps_ref[...]
        else:
            mix = _dot(o_ref[...], wao_ref[...]) + bao_ref[...]
        h1 = _layer_norm(ALPHA * x + mix, lmg_ref[...], lmb_ref[...])
        h1_ref[...] = h1
        h1b_ref[...] = h1.astype(BF16)
        acc_ref[...] = jnp.zeros_like(acc_ref)

    h1b = h1b_ref[...]
    g = _dot(h1b, wg_ref[...])
    u = _dot(h1b, wu_ref[...])
    g_ref[...] = g
    cc = cb_ref[...] + s0_ref[...] * cw_ref[0:1, :] + s1_ref[...] * cw_ref[1:2, :] + g * cw_ref[2:3, :]
    acc_ref[...] += _dot((_silu(cc) * u).astype(BF16), wo_ref[...])

    @pl.when(c == N_DEC_CHUNKS - 1)
    def _():
        h2 = _layer_norm(ALPHA * h1_ref[...] + acc_ref[...], lfg_ref[...], lfb_ref[...])
        y_ref[...] = h2
        h2b = h2.astype(BF16)
        if emit_kv:
            kv_ref[...] = _dot(h2b, wkv_ref[...]) + bkv_ref[...]
        if emit_q:
            q_ref[...] = _dot(h2b, wq_ref[...]) + bq_ref[...]


def _sample_layer(x, *, layer, kind, emit_kv, emit_q, name, mixer, ln, ffn, conv_t, kv=None, qproj=None):
    n = x.shape[0]
    ins, specs = [x], [_const_spec(x)]

    def add(a, spec):
        ins.append(a)
        specs.append(spec)

    if kind == "pool":
        for a in mixer:
            add(a, _layer_spec(a, layer))
    else:
        o, wao, bao = mixer
        add(o, _const_spec(o))
        add(wao, _layer_spec(wao, layer - N_A))
        add(bao, _layer_spec(bao, layer - N_A))
    for a in ln:
        add(a, _layer_spec(a, layer))
    wi, cw, cb, wo = ffn
    add(wi, pl.BlockSpec((D_MODEL, DEC_CHUNK), lambda c: (0, c)))
    add(wi, pl.BlockSpec((D_MODEL, DEC_CHUNK), lambda c: (0, N_DEC_CHUNKS + c)))
    add(cw, pl.BlockSpec((None, CONV_W, DEC_CHUNK), lambda c: (layer, 0, c)))
    add(cb, pl.BlockSpec((None, 1, DEC_CHUNK), lambda c: (layer, 0, c)))
    add(wo, pl.BlockSpec((DEC_CHUNK, D_MODEL), lambda c: (c, 0)))
    add(conv_t, pl.BlockSpec((None, None, n, DEC_CHUNK), lambda c: (layer, 0, 0, c)))
    add(conv_t, pl.BlockSpec((None, None, n, DEC_CHUNK), lambda c: (layer, 1, 0, c)))
    if emit_kv:
        for a in kv:
            add(a, _const_spec(a))
    if emit_q:
        for a in qproj:
            add(a, _layer_spec(a, layer + 1 - N_A))

    full = lambda w: pl.BlockSpec((n, w), lambda c: (0, 0))
    out_shape = [jax.ShapeDtypeStruct((n, D_MODEL), F32), jax.ShapeDtypeStruct((n, D_FF), F32)]
    out_specs = [full(D_MODEL), pl.BlockSpec((n, DEC_CHUNK), lambda c: (0, c))]
    if emit_kv:
        out_shape.append(jax.ShapeDtypeStruct((n, 2 * KV_DIM), F32))
        out_specs.append(full(2 * KV_DIM))
    if emit_q:
        out_shape.append(jax.ShapeDtypeStruct((n, D_MODEL), F32))
        out_specs.append(full(D_MODEL))
    body = functools.partial(_sample_kernel, kind=kind, emit_kv=emit_kv, emit_q=emit_q)
    return pl.pallas_call(
        body,
        grid=(N_DEC_CHUNKS,),
        in_specs=specs,
        out_specs=out_specs,
        out_shape=out_shape,
        scratch_shapes=[pltpu.VMEM((n, D_MODEL), F32), pltpu.VMEM((n, D_MODEL), BF16),
                        pltpu.VMEM((n, D_MODEL), F32)],
        compiler_params=pltpu.CompilerParams(dimension_semantics=("arbitrary",),
                                             vmem_limit_bytes=VMEM_LIMIT),
        name=name,
    )(*ins)


def _sample_attn_kernel(*refs, update):
    it = iter(refs)
    q_ref, kt_ref, vt_ref = next(it), next(it), next(it)
    if update:
        kn_ref, vn_ref = next(it), next(it)
    sink_ref, o_ref = next(it), next(it)
    if update:
        nkt_ref, nvt_ref = next(it), next(it)

    kt, vt = kt_ref[...], vt_ref[...]
    if update:
        newest = lax.broadcasted_iota(jnp.int32, (KV_DIM, WINDOW), 1) == WINDOW - 1

        def advance(win, new_ref):
            return jnp.stack([jnp.where(newest, new_ref[:, b:b + 1], pltpu.roll(win[b], WINDOW - 1, 1))
                              for b in range(SAMPLE_GROUP)])

        kt, vt = advance(kt, kn_ref), advance(vt, vn_ref)
        nkt_ref[...] = kt
        nvt_ref[...] = vt
    q = q_ref[...]
    s = jnp.einsum("bhc,bcj->bhj", q, kt.astype(BF16), preferred_element_type=F32) * ATTN_SCALE
    sink = sink_ref[...][None]
    m = jnp.max(jnp.maximum(s, sink), axis=-1, keepdims=True)
    e = jnp.exp(s - m)
    den = jnp.sum(e, axis=-1, keepdims=True) + jnp.exp(sink - m)
    p = (e / den).astype(BF16)
    o_ref[...] = jnp.einsum("bhj,bcj->bhc", p, vt.astype(BF16), preferred_element_type=F32).astype(BF16)


def _sample_attn(qexp, kt, vt, sinkb, layer, name, new_cols=None):
    n = qexp.shape[0]
    G = SAMPLE_GROUP
    update = new_cols is not None
    win_spec = pl.BlockSpec((G, KV_DIM, WINDOW), lambda i: (i, 0, 0))
    q_spec = pl.BlockSpec((G, N_HEADS, KV_DIM), lambda i: (i, 0, 0))
    ins, specs = [qexp, kt, vt], [q_spec, win_spec, win_spec]
    if update:
        col_spec = pl.BlockSpec((None, KV_DIM, G), lambda i: (i, 0, 0))
        ins += list(new_cols)
        specs += [col_spec, col_spec]
    ins.append(sinkb)
    specs.append(pl.BlockSpec((None, N_HEADS, WINDOW), lambda i: (layer - N_A, 0, 0)))
    out_shape = [jax.ShapeDtypeStruct((n, N_HEADS, KV_DIM), BF16)]
    out_specs = [q_spec]
    if update:
        out_shape += [jax.ShapeDtypeStruct(kt.shape, F32)] * 2
        out_specs += [win_spec, win_spec]
    return pl.pallas_call(
        functools.partial(_sample_attn_kernel, update=update),
        grid=(n // G,),
        in_specs=specs,
        out_specs=out_specs,
        out_shape=out_shape,
        compiler_params=pltpu.CompilerParams(dimension_semantics=("arbitrary",)),
        name=name,
    )(*ins)


def _dup_heads(w):
    lead = w.shape[:-1]
    w = w.reshape(*lead, N_KV_HEADS, 1, HEAD_DIM)
    return jnp.broadcast_to(w, (*lead, N_KV_HEADS, 2, HEAD_DIM)).reshape(*lead, KV_DUP)


def _expand_q(q):
    n = q.shape[0]
    q = q.reshape(n, N_KV_HEADS, GROUP, HEAD_DIM)
    parts = [jnp.pad(q[:, h], ((0, 0), (0, 0), (h * HEAD_DIM, KV_DIM - (h + 1) * HEAD_DIM)))
             for h in range(N_KV_HEADS)]
    return jnp.stack(parts, axis=1).reshape(n, N_HEADS, KV_DIM)


def _contract_o(res):
    n = res.shape[0]
    res = res.reshape(n, N_KV_HEADS, GROUP, N_KV_HEADS, HEAD_DIM)
    return jnp.stack([res[:, h, :, h, :] for h in range(N_KV_HEADS)], axis=1).reshape(n, N_HEADS * HEAD_DIM)


def _band_masks():
    i = jnp.arange(BLOCK)[:, None]
    j = jnp.arange(2 * BLOCK)[None, :]
    band = (j > i) & (j <= i + BLOCK)
    masks = jnp.stack([band, band & (j >= PAD_ROWS), band & (j >= BLOCK + PAD_ROWS)])
    return jnp.where(masks, 0.0, NEG).astype(F32)


def kernel(x_prompt, x_sample, state_pool, state_conv, state_k_win, state_v_win, meta_tokens, pool_w, pool_scale, w_kv, b_kv, attn_w_q, attn_b_q, attn_sinks, attn_w_o, attn_b_o, ffn_w_in, ffn_conv_w, ffn_conv_b, ffn_w_out, ln_mix_g, ln_mix_b, ln_ffn_g, ln_ffn_b):
    B, S, _ = x_prompt.shape
    n_dec = x_sample.shape[0]
    assert x_sample.shape[1] == 1 and S % ROW_TILE == 0 and n_dec % SAMPLE_GROUP == 0

    row = lambda a: a[:, None, :]
    wi_b, wo_b = ffn_w_in[0].astype(BF16), ffn_w_out[0].astype(BF16)
    conv_b = row(ffn_conv_b)
    ln =(row(ln_mix_g), row(ln_mix_b), row(ln_ffn_g), row(ln_ffn_b))
    pool = (pool_w.astype(BF16), row(pool_scale))
    wkv_b, bkv = w_kv.astype(BF16), b_kv[None, :]
    wkvd_b = jnp.concatenate([_dup_heads(w_kv[:, :KV_DIM]), _dup_heads(w_kv[:, KV_DIM:])], axis=1).astype(BF16)
    bkvd = jnp.concatenate([_dup_heads(b_kv[:KV_DIM]), _dup_heads(b_kv[KV_DIM:])])[None, :]
    kvw = (wkvd_b, bkvd, wkv_b, bkv)
    wq_b, bq = attn_w_q.astype(BF16), row(attn_b_q)
    wao_b, bao = attn_w_o.astype(BF16), row(attn_b_o)
    sinkb = jnp.broadcast_to(attn_sinks[:, :, None], (attn_sinks.shape[0], N_HEADS, WINDOW))
    masks = _band_masks()

    zeros_halo = jnp.zeros((MAX_POOL, D_MODEL), F32)
    zeros_gcar = jnp.zeros((SUBLANES, D_FF), F32)
    zeros_kv = jnp.zeros((BLOCK, KV_DUP), BF16)

    hm = jnp.concatenate([jnp.zeros((PAD_ROWS, D_MODEL), F32), meta_tokens.astype(F32)])[None]
    hp = x_prompt
    hs = x_sample.reshape(n_dec, D_MODEL)
    pool_t = jnp.transpose(state_pool, (0, 2, 1, 3))
    conv_t = jnp.transpose(state_conv, (0, 2, 1, 3))
    kt = jnp.transpose(state_k_win, (0, 2, 3, 1)).reshape(n_dec, KV_DIM, WINDOW)
    vt = jnp.transpose(state_v_win, (0, 2, 3, 1)).reshape(n_dec, KV_DIM, WINDOW)

    new_pool_p, new_pool_t, new_conv_p, new_conv_t = [], [], [], []
    for l in range(DEPTH):
        ffn = (wi_b, ffn_conv_w, conv_b, wo_b)
        cast_next = (ffn_w_in, ffn_w_out) if l + 1 < DEPTH else ()
        if l < N_A:
            emit_kv = l == N_A - 1
            new_pool_p.append(hp[:, S - POOL_STATE:])
            new_pool_t.append(jnp.concatenate([pool_t[l, 1:], hs[None]], axis=0))
            outs_m = _prompt_layer(hm, layer=l, kind="pool", is_meta=True, emit_kv=emit_kv, tile=BLOCK,
                                   name=f"meta_l{l}", halo_in=zeros_halo, gcar_in=zeros_gcar,
                                   pool=pool, ln=ln, ffn=ffn, kv=kvw)
            outs_p = _prompt_layer(hp, layer=l, kind="pool", is_meta=False, emit_kv=emit_kv, tile=ROW_TILE,
                                   name=f"prompt_l{l}", halo_in=hm[0, PAD_ROWS:], gcar_in=outs_m[1][0],
                                   pool=pool, ln=ln, ffn=ffn, kv=kvw, cast_next=cast_next)
            outs_s = _sample_layer(hs, layer=l, kind="pool", emit_kv=emit_kv, emit_q=emit_kv,
                                   name=f"sample_l{l}", mixer=(pool_t, *pool), ln=ln, ffn=ffn, conv_t=conv_t,
                                   kv=(wkv_b, bkv), qproj=(wq_b, bq))
            if emit_kv:
                kd_m, vd_m = outs_m[2][0], outs_m[3][0]
                kd_p, vd_p, kvtail = outs_p[2], outs_p[3], outs_p[4]
                kv_s, q_s = outs_s[2], outs_s[3]
        else:
            attn_w = (masks, attn_sinks, wq_b, bq, wao_b, bao)
            outs_m = _prompt_layer(hm, layer=l, kind="attn", is_meta=True, emit_kv=False, tile=BLOCK,
                                   name=f"meta_l{l}", gcar_in=zeros_gcar,
                                   attn=(kd_m[None], vd_m[None], zeros_kv, zeros_kv, *attn_w), ln=ln, ffn=ffn)
            outs_p = _prompt_layer(hp, layer=l, kind="attn", is_meta=False, emit_kv=False, tile=ROW_TILE,
                                   name=f"prompt_l{l}", gcar_in=outs_m[1][0],
                                   attn=(kd_p, vd_p, kd_m, vd_m, *attn_w), ln=ln, ffn=ffn,
                                   cast_next=cast_next)
            qexp = _expand_q(q_s.astype(BF16))
            if l == N_A:
                cols = lambda a: jnp.transpose(a.reshape(n_dec // SAMPLE_GROUP, SAMPLE_GROUP, KV_DIM), (0, 2, 1))
                res, kt, vt = _sample_attn(qexp, kt, vt, sinkb, l, f"sample_attn_l{l}",
                                           new_cols=(cols(kv_s[:, :KV_DIM]), cols(kv_s[:, KV_DIM:])))
            else:
                res = _sample_attn(qexp, kt, vt, sinkb, l, f"sample_attn_l{l}")[0]
            emit_q = l + 1 < DEPTH
            outs_s = _sample_layer(hs, layer=l, kind="attn", emit_kv=False, emit_q=emit_q, name=f"sample_l{l}",
                                   mixer=(_contract_o(res), wao_b, bao), ln=ln, ffn=ffn, conv_t=conv_t,
                                   qproj=(wq_b, bq))
            if emit_q:
                q_s = outs_s[2]
        new_conv_p.append(outs_p[1][:, SUBLANES - (CONV_W - 1):])
        new_conv_t.append(jnp.stack([conv_t[l, 1], outs_s[1]]))
        hm, hp, hs = outs_m[0], outs_p[0], outs_s[0]
        if cast_next:
            wi_b = outs_p[-2].reshape(D_MODEL, 2 * D_FF)
            wo_b = outs_p[-1].reshape(D_FF, D_MODEL)

    unwin = lambda t: jnp.transpose(t.reshape(-1, N_KV_HEADS, HEAD_DIM, WINDOW), (0, 3, 1, 2))
    new_k_p = kvtail[:, :, :KV_DIM].reshape(B, WINDOW, N_KV_HEADS, HEAD_DIM)
    new_v_p = kvtail[:, :, KV_DIM:].reshape(B, WINDOW, N_KV_HEADS, HEAD_DIM)
    return (hp, hs.reshape(n_dec, 1, D_MODEL),
            jnp.stack(new_pool_p), jnp.transpose(jnp.stack(new_pool_t), (0, 2, 1, 3)),
            jnp.stack(new_conv_p), jnp.transpose(jnp.stack(new_conv_t), (0, 2, 1, 3)),
            new_k_p, new_v_p, unwin(kt), unwin(vt))
```

```python
import functools

import jax
import jax.numpy as jnp
from jax import lax
from jax.experimental import pallas as pl
from jax.experimental.pallas import tpu as pltpu

F32 = jnp.float32
BF16 = jnp.bfloat16

D_MODEL = 1024
DEPTH = 4
N_META = 16
N_A = DEPTH // 2
POOL_WINDOWS = (2, 4, 8, 16)
N_POOL_GROUPS = len(POOL_WINDOWS)
POOL_GC = D_MODEL // N_POOL_GROUPS
MAX_POOL = max(POOL_WINDOWS)
POOL_STATE = MAX_POOL - 1
HEAD_DIM = 64
N_HEADS = D_MODEL // HEAD_DIM
N_KV_HEADS = 4
GROUP = N_HEADS // N_KV_HEADS
KV_DIM = N_KV_HEADS * HEAD_DIM
WINDOW = 128
BLOCK = 128
ATTN_SCALE = HEAD_DIM ** -0.5
D_FF = 2816
CONV_W = 3
ALPHA = (2.0 * DEPTH) ** 0.25
LN_EPS = 1e-5
NEG = -1e30

SUBLANES = 8
LANES = 128
PAD_ROWS = BLOCK - N_META
FF_CHUNK = 256
N_FF_CHUNKS = D_FF // FF_CHUNK
OUT_BLOCK = 256
N_OUT_BLOCKS = D_MODEL // OUT_BLOCK
DEC_CHUNK = D_FF // 2
N_DEC_CHUNKS = D_FF // DEC_CHUNK
CAST_SLABS = 16
DOWN_ROWS = 256
GATE_ROWS = 128
ROW_TILE = 512
SAMPLE_GROUP = 16
DUP = 2 * HEAD_DIM
KV_DUP = N_KV_HEADS * DUP
VMEM_LIMIT = 58 * 1024 * 1024


def _layer_norm(z, g, b):
    mu = jnp.mean(z, axis=-1, keepdims=True)
    zc = z - mu
    var = jnp.mean(zc * zc, axis=-1, keepdims=True)
    return zc * lax.rsqrt(var + LN_EPS) * g + b


def _silu(c):
    return c * (1.0 / (1.0 + jnp.exp(-c)))


def _dot(a, b):
    return jnp.dot(a, b, preferred_element_type=F32)


def _const_spec(a):
    nd = a.ndim
    return pl.BlockSpec(a.shape, lambda *_: (0,) * nd, pipeline_mode=pl.Buffered(1))


def _layer_spec(a, l):
    nd = a.ndim
    return pl.BlockSpec((None,) + a.shape[1:], lambda *_: (l,) + (0,) * (nd - 1),
                        pipeline_mode=pl.Buffered(1))


def _prompt_kernel(*refs, kind, is_meta, emit_kv, tile, n_tiles, n_steps, attn_idx, n_cast):
    it = iter(refs)
    x_ref = next(it)
    halo_in_ref = next(it) if kind == "pool" else None
    gcar_in_ref = next(it)
    if kind == "pool":
        pw_ref, ps_ref = next(it), next(it)
    else:
        kcur_ref, vcur_ref, kprev_ref, vprev_ref, kmeta_ref, vmeta_ref = (next(it) for _ in range(6))
        mask_ref, sinks_ref = next(it), next(it)
        wq_ref, bq_ref, wao_ref, bao_ref = (next(it) for _ in range(4))
    lmg_ref, lmb_ref, lfg_ref, lfb_ref = (next(it) for _ in range(4))
    wi_ref, cw_ref, cb_ref, wo_ref = (next(it) for _ in range(4))
    if emit_kv:
        wkvd_ref, bkvd_ref, wkv_ref, bkv_ref = (next(it) for _ in range(4))
    cast_in_refs = [next(it) for _ in range(n_cast)]
    y_ref, gtail_ref = next(it), next(it)
    if emit_kv:
        kd_ref, vd_ref, kvtail_ref = next(it), next(it), next(it)
    for src_ref in cast_in_refs:
        next(it)[...] = src_ref[...].astype(BF16)
    gcar_ref, hh_ref, h1_ref, hb_ref = (next(it) for _ in range(4))
    if kind == "pool":
        halo_ref = next(it)
    else:
        qb_ref, ob_ref = next(it), next(it)
    if emit_kv:
        h2b_ref = next(it)

    T = tile
    n_rb = T // BLOCK
    s = pl.program_id(0)
    i = s % n_tiles
    has_cur = s < n_steps - 1

    @pl.when(s == 0)
    def _():
        hh_ref[...] = jnp.zeros_like(hh_ref)
        h1_ref[...] = jnp.zeros_like(h1_ref)

    @pl.when(jnp.logical_and(has_cur, i == 0))
    def _():
        gcar_ref[...] = gcar_in_ref[...]
        if kind == "pool":
            halo_ref[...] = halo_in_ref[...]

    def block_rows(r):
        return slice(r * BLOCK, (r + 1) * BLOCK)

    def live_rows(r):
        return lax.broadcasted_iota(jnp.int32, (BLOCK, 1), 0) + r * BLOCK >= PAD_ROWS

    down_rows = min(DOWN_ROWS, T)
    down_ids = [(n, m) for m in range(T // down_rows) for n in range(N_OUT_BLOCKS)]

    def down_piece(f_pieces, n, m):
        f_pieces[n, m] = _dot(hh_ref[m * down_rows:(m + 1) * down_rows, :],
                              wo_ref[:, n * OUT_BLOCK:(n + 1) * OUT_BLOCK])

    def finish_rows(r, f_pieces, h1_old_ref):
        rows = block_rows(r)
        m, off = divmod(r * BLOCK, down_rows)
        f = jnp.concatenate([f_pieces[n, m][off:off + BLOCK] for n in range(N_OUT_BLOCKS)], axis=1)
        h2 = _layer_norm(ALPHA * h1_old_ref[rows, :] + f, lfg_ref[...], lfb_ref[...])
        y_ref[0, rows, :] = h2
        if emit_kv:
            h2b_ref[rows, :] = h2.astype(BF16)

    def kv_block(n):
        cols = slice(n * OUT_BLOCK, (n + 1) * OUT_BLOCK)
        kvd = (_dot(h2b_ref[...], wkvd_ref[:, cols]) + bkvd_ref[:, cols]).astype(BF16)
        if n * OUT_BLOCK < KV_DUP:
            kd_ref[0, :, cols] = kvd
        else:
            vd_ref[0, :, n * OUT_BLOCK - KV_DUP:(n + 1) * OUT_BLOCK - KV_DUP] = kvd

    def first_norm(r, x_blk, mix, h1_new_ref):
        rows = block_rows(r)
        h1 = _layer_norm(ALPHA * x_blk + mix, lmg_ref[...], lmb_ref[...])
        h1_new_ref[rows, :] = h1
        hb_ref[rows, :] = h1.astype(BF16)

    def pool_rows(r, h1_new_ref):
        rows = block_rows(r)
        x_blk = x_ref[0, rows, :]
        if is_meta:
            x_blk = jnp.where(live_rows(r), x_blk, 0.0)
        if r == 0:
            xe = jnp.concatenate([halo_ref[...], x_blk], axis=0)
        else:
            xe = x_ref[0, r * BLOCK - MAX_POOL:(r + 1) * BLOCK, :]
        s2 = xe + pltpu.roll(xe, 1, 0)
        s2r = s2[:, POOL_GC:]
        s4 = s2r + pltpu.roll(s2r, 2, 0)
        s4r = s4[:, POOL_GC:]
        s8 = s4r + pltpu.roll(s4r, 4, 0)
        s8r = s8[:, POOL_GC:]
        s16 = s8r + pltpu.roll(s8r, 8, 0)
        wins = (s2[MAX_POOL:, :POOL_GC], s4[MAX_POOL:, :POOL_GC],
                s8[MAX_POOL:, :POOL_GC], s16[MAX_POOL:])
        parts = []
        for gi, w in enumerate(POOL_WINDOWS):
            xg = x_blk[:, gi * POOL_GC:(gi + 1) * POOL_GC]
            if is_meta:
                t1 = lax.broadcasted_iota(jnp.int32, (BLOCK, 1), 0) + (r * BLOCK - PAD_ROWS + 1)
                dg = wins[gi] / jnp.clip(t1, 1, w).astype(F32) - xg
            else:
                dg = wins[gi] * (1.0 / w) - xg
            parts.append(_dot(dg.astype(BF16), pw_ref[gi]))
        mix = jnp.concatenate(parts, axis=1) * ps_ref[...]
        first_norm(r, x_blk, mix, h1_new_ref)

    def q_rows(r):
        rows = block_rows(r)
        q = (_dot(x_ref[0, rows, :].astype(BF16), wq_ref[...]) + bq_ref[...]) * ATTN_SCALE
        qb_ref[rows, :] = q.astype(BF16)

    lo = lax.broadcasted_iota(jnp.int32, (BLOCK, LANES), 1) < HEAD_DIM

    def keys_values(j, h):
        cols = slice(h * DUP, (h + 1) * DUP)
        if j == 0:
            first = i == 0
            kp = jnp.where(first, kmeta_ref[:, cols], kprev_ref[0, :, cols])
            vp = jnp.where(first, vmeta_ref[:, cols], vprev_ref[0, :, cols])
            return (jnp.concatenate([kp, kcur_ref[0, 0:BLOCK, cols]], axis=0),
                    jnp.concatenate([vp, vcur_ref[0, 0:BLOCK, cols]], axis=0))
        return (kcur_ref[0, (j - 1) * BLOCK:(j + 1) * BLOCK, cols],
                vcur_ref[0, (j - 1) * BLOCK:(j + 1) * BLOCK, cols])

    def scores(j, h):
        rows = block_rows(j)
        c0 = h * GROUP * HEAD_DIM
        qa = qb_ref[rows, c0:c0 + LANES]
        qc = qb_ref[rows, c0 + LANES:c0 + 2 * LANES]
        z = jnp.zeros_like(qa)
        lhs = jnp.concatenate([jnp.where(lo, qa, z), jnp.where(lo, z, qa),
                               jnp.where(lo, qc, z), jnp.where(lo, z, qc)], axis=0)
        kk, vv = keys_values(j, h)
        return lax.dot_general(lhs, kk, (((1,), (1,)), ((), ())), preferred_element_type=F32), vv

    def softmax(j, h, sc):
        if is_meta:
            madd = mask_ref[2]
        elif j == 0:
            madd = mask_ref[jnp.where(i == 0, 1, 0)]
        else:
            madd = mask_ref[0]
        ps = []
        for g in range(GROUP):
            sg = sc[g * BLOCK:(g + 1) * BLOCK] + madd
            sink = sinks_ref[attn_idx, h * GROUP + g]
            m = jnp.maximum(jnp.max(sg, axis=-1, keepdims=True), sink)
            e = jnp.exp(sg - m)
            den = jnp.sum(e, axis=-1, keepdims=True) + jnp.exp(sink - m)
            ps.append((e / den).astype(BF16))
        return jnp.concatenate(ps, axis=0)

    def weighted_values(j, h, p_pair, v_pair):
        rows = block_rows(j)
        res2 = _dot(jnp.concatenate(p_pair, axis=0), jnp.concatenate(v_pair, axis=1))
        for k in range(2):
            c0 = (h + k) * GROUP * HEAD_DIM
            res = res2[k * GROUP * BLOCK:(k + 1) * GROUP * BLOCK, k * DUP:(k + 1) * DUP]
            oa = jnp.where(lo, res[0:BLOCK], res[BLOCK:2 * BLOCK])
            oc = jnp.where(lo, res[2 * BLOCK:3 * BLOCK], res[3 * BLOCK:])
            ob_ref[rows, c0:c0 + LANES] = oa.astype(BF16)
            ob_ref[rows, c0 + LANES:c0 + 2 * LANES] = oc.astype(BF16)

    def attn_out_rows(r, h1_new_ref):
        rows = block_rows(r)
        mix = _dot(ob_ref[rows, :], wao_ref[...]) + bao_ref[...]
        first_norm(r, x_ref[0, rows, :], mix, h1_new_ref)

    gu_rows = min(GATE_ROWS, T)
    gu_ids = [(c, m) for c in range(N_FF_CHUNKS) for m in range(T // gu_rows)]

    def gate_up(c, m):
        hb = hb_ref[m * gu_rows:(m + 1) * gu_rows, :]
        return (_dot(hb, wi_ref[:, c * FF_CHUNK:(c + 1) * FF_CHUNK]),
                _dot(hb, wi_ref[:, D_FF + c * FF_CHUNK:D_FF + (c + 1) * FF_CHUNK]))

    def conv_rows(c, m, g, u):
        cols = slice(c * FF_CHUNK, (c + 1) * FF_CHUNK)
        if is_meta:
            g = jnp.where(jnp.concatenate([live_rows(r) for r in range(n_rb)], axis=0), g, 0.0)
        gext = jnp.concatenate([gcar_ref[:, cols], g], axis=0)
        g1 = pltpu.roll(gext, 1, 0)[SUBLANES:]
        g2 = pltpu.roll(gext, 2, 0)[SUBLANES:]
        gcar_ref[:, cols] = g[gu_rows - SUBLANES:]
        cc = (cb_ref[:, cols] + g2 * cw_ref[0:1, cols] + g1 * cw_ref[1:2, cols]
              + g * cw_ref[2:3, cols])
        hh_ref[m * gu_rows:(m + 1) * gu_rows, cols] = (_silu(cc) * u).astype(BF16)

    def body(h1_new_ref, h1_old_ref):
        f_pieces = {}
        downs = [functools.partial(down_piece, f_pieces, n, m) for n, m in down_ids]
        if kind == "pool":
            per = -(-len(downs) // n_rb)
            for r in range(n_rb):
                for _ in range(min(per, len(downs))):
                    downs.pop(0)()
                pool_rows(r, h1_new_ref)
            tail = x_ref[0, T - MAX_POOL:, :]
            if is_meta:
                tail = jnp.where(live_rows(n_rb - 1)[BLOCK - MAX_POOL:], tail, 0.0)
            halo_ref[...] = tail
        else:
            for r in range(n_rb):
                q_rows(r)
            units = [(j, h) for j in range(n_rb) for h in range(N_KV_HEADS)]
            per = -(-len(downs) // len(units))
            stride = max(len(units) // len(downs), 1)
            nxt = scores(*units[0])
            for p, (j, h) in enumerate(units):
                sc, vv = nxt
                if p + 1 < len(units):
                    nxt = scores(*units[p + 1])
                if p % stride == min(1, stride - 1):
                    for _ in range(min(per, len(downs))):
                        downs.pop(0)()
                if h % 2 == 0:
                    held = (softmax(j, h, sc), vv)
                else:
                    weighted_values(j, h - 1, (held[0], softmax(j, h, sc)), (held[1], vv))
                if h == N_KV_HEADS - 1:
                    attn_out_rows(j, h1_new_ref)
        for job in downs:
            job()

        under = [functools.partial(finish_rows, r, f_pieces, h1_old_ref) for r in range(n_rb)]
        if emit_kv:
            under += [functools.partial(kv_block, n) for n in range(2 * KV_DUP // OUT_BLOCK)]
        nxt = gate_up(*gu_ids[0])
        for k, (c, m) in enumerate(gu_ids):
            g, u = nxt
            if k + 1 < len(gu_ids):
                nxt = gate_up(*gu_ids[k + 1])
            conv_rows(c, m, g, u)
            if under and (k + 1) % (T // gu_rows) == 0:
                under.pop(0)()
        for job in under:
            job()
        gtail_ref[0] = gcar_ref[...]

    @pl.when(has_cur)
    def _():
        body(h1_ref.at[s % 2], h1_ref.at[1 - s % 2])

    @pl.when(s == n_steps - 1)
    def _():
        h1_old_ref = h1_ref.at[(n_steps - 2) % 2]
        f_pieces = {}
        for n, m in down_ids:
            down_piece(f_pieces, n, m)
        for r in range(n_rb):
            finish_rows(r, f_pieces, h1_old_ref)
        if emit_kv:
            for n in range(2 * KV_DUP // OUT_BLOCK):
                kv_block(n)

    if emit_kv:
        @pl.when(jnp.logical_and(s > 0, (s - 1) % n_tiles == n_tiles - 1))
        def _():
            kvtail_ref[0] = _dot(h2b_ref[T - WINDOW:, :], wkv_ref[...]) + bkv_ref[...]


def _prompt_layer(x, *, layer, kind, is_meta, emit_kv, tile, name, halo_in=None, gcar_in,
                  pool=None, attn=None, ln, ffn, kv=None, cast_next=()):
    B, L, _ = x.shape
    n_tiles = L // tile
    n_all = B * n_tiles
    n_steps = n_all + 1
    nb = tile // BLOCK
    attn_idx = layer - N_A

    def cur(s):
        t = jnp.minimum(s, n_all - 1)
        return t // n_tiles, t % n_tiles

    def prev(s):
        t = jnp.maximum(s - 1, 0)
        return t // n_tiles, t % n_tiles

    cur_rows = lambda w: pl.BlockSpec((1, tile, w), lambda s: (*cur(s), 0))
    prev_rows = lambda w: pl.BlockSpec((1, tile, w), lambda s: (*prev(s), 0))
    ins, specs = [x], [cur_rows(D_MODEL)]

    def add(a, spec):
        ins.append(a)
        specs.append(spec)

    if kind == "pool":
        add(halo_in, _const_spec(halo_in))
    add(gcar_in, _const_spec(gcar_in))
    if kind == "pool":
        for a in pool:
            add(a, _layer_spec(a, layer))
    else:
        kd, vd, kmeta, vmeta, mask, sinks, wq, bq, wao, bao = attn

        def before(s):
            b, i = cur(s)
            return b, jnp.maximum(i * nb - 1, 0), 0

        before_spec = pl.BlockSpec((1, BLOCK, KV_DUP), before)
        ins += [kd, vd, kd, vd]
        specs += [cur_rows(KV_DUP), cur_rows(KV_DUP), before_spec, before_spec]
        for a in (kmeta, vmeta, mask):
            add(a, _const_spec(a))
        add(sinks, pl.BlockSpec(memory_space=pltpu.SMEM))
        for a in (wq, bq, wao, bao):
            add(a, _layer_spec(a, attn_idx))
    wi, cw, cb, wo = ffn
    for a in ln:
        add(a, _layer_spec(a, layer))
    add(wi, _const_spec(wi))
    add(cw, _layer_spec(cw, layer))
    add(cb, _layer_spec(cb, layer))
    add(wo, _const_spec(wo))
    if emit_kv:
        for a in kv:
            add(a, _const_spec(a))
    for a in cast_next:
        slab = a.shape[1] // CAST_SLABS
        add(a.reshape(a.shape[0], CAST_SLABS, slab, a.shape[2]),
            pl.BlockSpec((None, None, slab, a.shape[2]),
                         lambda s: (layer + 1, jnp.minimum(s, CAST_SLABS - 1), 0, 0)))

    out_shape = [jax.ShapeDtypeStruct((B, L, D_MODEL), F32),
                 jax.ShapeDtypeStruct((B, SUBLANES, D_FF), F32)]
    out_specs = [prev_rows(D_MODEL), pl.BlockSpec((1, SUBLANES, D_FF), lambda s: (cur(s)[0], 0, 0))]
    if emit_kv:
        out_shape += [jax.ShapeDtypeStruct((B, L, KV_DUP), BF16),
                      jax.ShapeDtypeStruct((B, L, KV_DUP), BF16),
                      jax.ShapeDtypeStruct((B, WINDOW, 2 * KV_DIM), F32)]
        out_specs += [prev_rows(KV_DUP), prev_rows(KV_DUP),
                      pl.BlockSpec((1, WINDOW, 2 * KV_DIM), lambda s: (prev(s)[0], 0, 0))]
    for a in cast_next:
        slab = a.shape[1] // CAST_SLABS
        out_shape.append(jax.ShapeDtypeStruct((CAST_SLABS, slab, a.shape[2]), BF16))
        out_specs.append(pl.BlockSpec((None, slab, a.shape[2]),
                                      lambda s: (jnp.minimum(s, CAST_SLABS - 1), 0, 0)))
    scratch = [pltpu.VMEM((SUBLANES, D_FF), F32), pltpu.VMEM((tile, D_FF), BF16),
               pltpu.VMEM((2, tile, D_MODEL), F32), pltpu.VMEM((tile, D_MODEL), BF16)]
    if kind == "pool":
        scratch.append(pltpu.VMEM((MAX_POOL, D_MODEL), F32))
    else:
        scratch += [pltpu.VMEM((tile, D_MODEL), BF16), pltpu.VMEM((tile, D_MODEL), BF16)]
    if emit_kv:
        scratch.append(pltpu.VMEM((tile, D_MODEL), BF16))
    body = functools.partial(_prompt_kernel, kind=kind, is_meta=is_meta, emit_kv=emit_kv,
                             tile=tile, n_tiles=n_tiles, n_steps=n_steps, attn_idx=attn_idx,
                             n_cast=len(cast_next))
    return pl.pallas_call(
        body,
        grid=(n_steps,),
        in_specs=specs,
        out_specs=out_specs,
        out_shape=out_shape,
        scratch_shapes=scratch,
        compiler_params=pltpu.CompilerParams(
            dimension_semantics=("arbitrary",), vmem_limit_bytes=VMEM_LIMIT),
        name=name,
    )(*ins)


def _sample_kernel(*refs, kind, emit_kv, emit_q):
    it = iter(refs)
    x_ref = next(it)
    if kind == "pool":
        sp_ref, pw_ref, ps_ref = next(it), next(it), next(it)
    else:
        o_ref, wao_ref, bao_ref = next(it), next(it), next(it)
    lmg_ref, lmb_ref, lfg_ref, lfb_ref = (next(it) for _ in range(4))
    wg_ref, wu_ref, cw_ref, cb_ref, wo_ref, s0_ref, s1_ref = (next(it) for _ in range(7))
    if emit_kv:
        wkv_ref, bkv_ref = next(it), next(it)
    if emit_q:
        wq_ref, bq_ref = next(it), next(it)
    y_ref, g_ref = next(it), next(it)
    if emit_kv:
        kv_ref = next(it)
    if emit_q:
        q_ref = next(it)
    h1_ref, h1b_ref, acc_ref = next(it), next(it), next(it)

    c = pl.program_id(0)

    @pl.when(c == 0)
    def _():
        x = x_ref[...]
        if kind == "pool":
            parts = []
            for gi, w in enumerate(POOL_WINDOWS):
                cols = slice(gi * POOL_GC, (gi + 1) * POOL_GC)
                win = x[:, cols]
                for k in range(1, w):
                    win = win + sp_ref[POOL_STATE - k, :, cols]
                dg = win * (1.0 / w) - x[:, cols]
                parts.append(_dot(dg.astype(BF16), pw_ref[gi]))
            mix = jnp.concatenate(parts, axis=1) * ps_ref[...]
        else:
            mix = _dot(o_ref[...], wao_ref[...]) + bao_ref[...]
        h1 = _layer_norm(ALPHA * x + mix, lmg_ref[...], lmb_ref[...])
        h1_ref[...] = h1
        h1b_ref[...] = h1.astype(BF16)
        acc_ref[...] = jnp.zeros_like(acc_ref)

    h1b = h1b_ref[...]
    g = _dot(h1b, wg_ref[...])
    u = _dot(h1b, wu_ref[...])
    g_ref[...] = g
    cc = cb_ref[...] + s0_ref[...] * cw_ref[0:1, :] + s1_ref[...] * cw_ref[1:2, :] + g * cw_ref[2:3, :]
    acc_ref[...] += _dot((_silu(cc) * u).astype(BF16), wo_ref[...])

    @pl.when(c == N_DEC_CHUNKS - 1)
    def _():
        h2 = _layer_norm(ALPHA * h1_ref[...] + acc_ref[...], lfg_ref[...], lfb_ref[...])
        y_ref[...] = h2
        h2b = h2.astype(BF16)
        if emit_kv:
            kv_ref[...] = _dot(h2b, wkv_ref[...]) + bkv_ref[...]
        if emit_q:
            q_ref[...] = _dot(h2b, wq_ref[...]) + bq_ref[...]


def _sample_layer(x, *, layer, kind, emit_kv, emit_q, name, mixer, ln, ffn, conv_t, kv=None, qproj=None):
    n = x.shape[0]
    ins, specs = [x], [_const_spec(x)]

    def add(a, spec):
        ins.append(a)
        specs.append(spec)

    if kind == "pool":
        for a in mixer:
            add(a, _layer_spec(a, layer))
    else:
        o, wao, bao = mixer
        add(o, _const_spec(o))
        add(wao, _layer_spec(wao, layer - N_A))
        add(bao, _layer_spec(bao, layer - N_A))
    for a in ln:
        add(a, _layer_spec(a, layer))
    wi, cw, cb, wo = ffn
    add(wi, pl.BlockSpec((D_MODEL, DEC_CHUNK), lambda c: (0, c)))
    add(wi, pl.BlockSpec((D_MODEL, DEC_CHUNK), lambda c: (0, N_DEC_CHUNKS + c)))
    add(cw, pl.BlockSpec((None, CONV_W, DEC_CHUNK), lambda c: (layer, 0, c)))
    add(cb, pl.BlockSpec((None, 1, DEC_CHUNK), lambda c: (layer, 0, c)))
    add(wo, pl.BlockSpec((DEC_CHUNK, D_MODEL), lambda c: (c, 0)))
    add(conv_t, pl.BlockSpec((None, None, n, DEC_CHUNK), lambda c: (layer, 0, 0, c)))
    add(conv_t, pl.BlockSpec((None, None, n, DEC_CHUNK), lambda c: (layer, 1, 0, c)))
    if emit_kv:
        for a in kv:
            add(a, _const_spec(a))
    if emit_q:
        for a in qproj:
            add(a, _layer_spec(a, layer + 1 - N_A))

    full = lambda w: pl.BlockSpec((n, w), lambda c: (0, 0))
    out_shape = [jax.ShapeDtypeStruct((n, D_MODEL), F32), jax.ShapeDtypeStruct((n, D_FF), F32)]
    out_specs = [full(D_MODEL), pl.BlockSpec((n, DEC_CHUNK), lambda c: (0, c))]
    if emit_kv:
        out_shape.append(jax.ShapeDtypeStruct((n, 2 * KV_DIM), F32))
        out_specs.append(full(2 * KV_DIM))
    if emit_q:
        out_shape.append(jax.ShapeDtypeStruct((n, D_MODEL), F32))
        out_specs.append(full(D_MODEL))
    body = functools.partial(_sample_kernel, kind=kind, emit_kv=emit_kv, emit_q=emit_q)
    return pl.pallas_call(
        body,
        grid=(N_DEC_CHUNKS,),
        in_specs=specs,
        out_specs=out_specs,
        out_shape=out_shape,
        scratch_shapes=[pltpu.VMEM((n, D_MODEL), F32), pltpu.VMEM((n, D_MODEL), BF16),
                        pltpu.VMEM((n, D_MODEL), F32)],
        compiler_params=pltpu.CompilerParams(dimension_semantics=("arbitrary",),
                                             vmem_limit_bytes=VMEM_LIMIT),
        name=name,
    )(*ins)


def _sample_attn_kernel(*refs, update):
    it = iter(refs)
    q_ref, kt_ref, vt_ref = next(it), next(it), next(it)
    if update:
        kn_ref, vn_ref = next(it), next(it)
    sink_ref, o_ref = next(it), next(it)
    if update:
        nkt_ref, nvt_ref = next(it), next(it)

    kt, vt = kt_ref[...], vt_ref[...]
    if update:
        newest = lax.broadcasted_iota(jnp.int32, (KV_DIM, WINDOW), 1) == WINDOW - 1

        def advance(win, new_ref):
            return jnp.stack([jnp.where(newest, new_ref[:, b:b + 1], pltpu.roll(win[b], WINDOW - 1, 1))
                              for b in range(SAMPLE_GROUP)])

        kt, vt = advance(kt, kn_ref), advance(vt, vn_ref)
        nkt_ref[...] = kt
        nvt_ref[...] = vt
    q = q_ref[...]
    s = jnp.einsum("bhc,bcj->bhj", q, kt.astype(BF16), preferred_element_type=F32) * ATTN_SCALE
    sink = sink_ref[...][None]
    m = jnp.max(jnp.maximum(s, sink), axis=-1, keepdims=True)
    e = jnp.exp(s - m)
    den = jnp.sum(e, axis=-1, keepdims=True) + jnp.exp(sink - m)
    p = (e / den).astype(BF16)
    o_ref[...] = jnp.einsum("bhj,bcj->bhc", p, vt.astype(BF16), preferred_element_type=F32).astype(BF16)


def _sample_attn(qexp, kt, vt, sinkb, layer, name, new_cols=None):
    n = qexp.shape[0]
    G = SAMPLE_GROUP
    update = new_cols is not None
    win_spec = pl.BlockSpec((G, KV_DIM, WINDOW), lambda i: (i, 0, 0))
    q_spec = pl.BlockSpec((G, N_HEADS, KV_DIM), lambda i: (i, 0, 0))
    ins, specs = [qexp, kt, vt], [q_spec, win_spec, win_spec]
    if update:
        col_spec = pl.BlockSpec((None, KV_DIM, G), lambda i: (i, 0, 0))
        ins += list(new_cols)
        specs += [col_spec, col_spec]
    ins.append(sinkb)
    specs.append(pl.BlockSpec((None, N_HEADS, WINDOW), lambda i: (layer - N_A, 0, 0)))
    out_shape = [jax.ShapeDtypeStruct((n, N_HEADS, KV_DIM), BF16)]
    out_specs = [q_spec]
    if update:
        out_shape += [jax.ShapeDtypeStruct(kt.shape, F32)] * 2
        out_specs += [win_spec, win_spec]
    return pl.pallas_call(
        functools.partial(_sample_attn_kernel, update=update),
        grid=(n // G,),
        in_specs=specs,
        out_specs=out_specs,
        out_shape=out_shape,
        compiler_params=pltpu.CompilerParams(dimension_semantics=("arbitrary",)),
        name=name,
    )(*ins)


def _dup_heads(w):
    lead = w.shape[:-1]
    w = w.reshape(*lead, N_KV_HEADS, 1, HEAD_DIM)
    return jnp.broadcast_to(w, (*lead, N_KV_HEADS, 2, HEAD_DIM)).reshape(*lead, KV_DUP)


def _expand_q(q):
    n = q.shape[0]
    q = q.reshape(n, N_KV_HEADS, GROUP, HEAD_DIM)
    parts = [jnp.pad(q[:, h], ((0, 0), (0, 0), (h * HEAD_DIM, KV_DIM - (h + 1) * HEAD_DIM)))
             for h in range(N_KV_HEADS)]
    return jnp.stack(parts, axis=1).reshape(n, N_HEADS, KV_DIM)


def _contract_o(res):
    n = res.shape[0]
    res = res.reshape(n, N_KV_HEADS, GROUP, N_KV_HEADS, HEAD_DIM)
    return jnp.stack([res[:, h, :, h, :] for h in range(N_KV_HEADS)], axis=1).reshape(n, N_HEADS * HEAD_DIM)


def _band_masks():
    i = jnp.arange(BLOCK)[:, None]
    j = jnp.arange(2 * BLOCK)[None, :]
    band = (j > i) & (j <= i + BLOCK)
    masks = jnp.stack([band, band & (j >= PAD_ROWS), band & (j >= BLOCK + PAD_ROWS)])
    return jnp.where(masks, 0.0, NEG).astype(F32)


def kernel(x_prompt, x_sample, state_pool, state_conv, state_k_win, state_v_win, meta_tokens, pool_w, pool_scale, w_kv, b_kv, attn_w_q, attn_b_q, attn_sinks, attn_w_o, attn_b_o, ffn_w_in, ffn_conv_w, ffn_conv_b, ffn_w_out, ln_mix_g, ln_mix_b, ln_ffn_g, ln_ffn_b):
    B, S, _ = x_prompt.shape
    n_dec = x_sample.shape[0]
    assert x_sample.shape[1] == 1 and S % ROW_TILE == 0 and n_dec % SAMPLE_GROUP == 0

    row = lambda a: a[:, None, :]
    wi_b, wo_b = ffn_w_in[0].astype(BF16), ffn_w_out[0].astype(BF16)
    conv_b = row(ffn_conv_b)
    ln =(row(ln_mix_g), row(ln_mix_b), row(ln_ffn_g), row(ln_ffn_b))
    pool = (pool_w.astype(BF16), row(pool_scale))
    wkv_b, bkv = w_kv.astype(BF16), b_kv[None, :]
    wkvd_b = jnp.concatenate([_dup_heads(w_kv[:, :KV_DIM]), _dup_heads(w_kv[:, KV_DIM:])], axis=1).astype(BF16)
    bkvd = jnp.concatenate([_dup_heads(b_kv[:KV_DIM]), _dup_heads(b_kv[KV_DIM:])])[None, :]
    kvw = (wkvd_b, bkvd, wkv_b, bkv)
    wq_b, bq = attn_w_q.astype(BF16), row(attn_b_q)
    wao_b, bao = attn_w_o.astype(BF16), row(attn_b_o)
    sinkb = jnp.broadcast_to(attn_sinks[:, :, None], (attn_sinks.shape[0], N_HEADS, WINDOW))
    masks = _band_masks()

    zeros_halo = jnp.zeros((MAX_POOL, D_MODEL), F32)
    zeros_gcar = jnp.zeros((SUBLANES, D_FF), F32)
    zeros_kv = jnp.zeros((BLOCK, KV_DUP), BF16)

    hm = jnp.concatenate([jnp.zeros((PAD_ROWS, D_MODEL), F32), meta_tokens.astype(F32)])[None]
    hp = x_prompt
    hs = x_sample.reshape(n_dec, D_MODEL)
    pool_t = jnp.transpose(state_pool, (0, 2, 1, 3))
    conv_t = jnp.transpose(state_conv, (0, 2, 1, 3))
    kt = jnp.transpose(state_k_win, (0, 2, 3, 1)).reshape(n_dec, KV_DIM, WINDOW)
    vt = jnp.transpose(state_v_win, (0, 2, 3, 1)).reshape(n_dec, KV_DIM, WINDOW)

    new_pool_p, new_pool_t, new_conv_p, new_conv_t = [], [], [], []
    for l in range(DEPTH):
        ffn = (wi_b, ffn_conv_w, conv_b, wo_b)
        cast_next = (ffn_w_in, ffn_w_out) if l + 1 < DEPTH else ()
        if l < N_A:
            emit_kv = l == N_A - 1
            new_pool_p.append(hp[:, S - POOL_STATE:])
            new_pool_t.append(jnp.concatenate([pool_t[l, 1:], hs[None]], axis=0))
            outs_m = _prompt_layer(hm, layer=l, kind="pool", is_meta=True, emit_kv=emit_kv, tile=BLOCK,
                                   name=f"meta_l{l}", halo_in=zeros_halo, gcar_in=zeros_gcar,
                                   pool=pool, ln=ln, ffn=ffn, kv=kvw)
            outs_p = _prompt_layer(hp, layer=l, kind="pool", is_meta=False, emit_kv=emit_kv, tile=ROW_TILE,
                                   name=f"prompt_l{l}", halo_in=hm[0, PAD_ROWS:], gcar_in=outs_m[1][0],
                                   pool=pool, ln=ln, ffn=ffn, kv=kvw, cast_next=cast_next)
            outs_s = _sample_layer(hs, layer=l, kind="pool", emit_kv=emit_kv, emit_q=emit_kv,
                                   name=f"sample_l{l}", mixer=(pool_t, *pool), ln=ln, ffn=ffn, conv_t=conv_t,
                                   kv=(wkv_b, bkv), qproj=(wq_b, bq))
            if emit_kv:
                kd_m, vd_m = outs_m[2][0], outs_m[3][0]
                kd_p, vd_p, kvtail = outs_p[2], outs_p[3], outs_p[4]
                kv_s, q_s = outs_s[2], outs_s[3]
        else:
            attn_w = (masks, attn_sinks, wq_b, bq, wao_b, bao)
            outs_m = _prompt_layer(hm, layer=l, kind="attn", is_meta=True, emit_kv=False, tile=BLOCK,
                                   name=f"meta_l{l}", gcar_in=zeros_gcar,
                                   attn=(kd_m[None], vd_m[None], zeros_kv, zeros_kv, *attn_w), ln=ln, ffn=ffn)
            outs_p = _prompt_layer(hp, layer=l, kind="attn", is_meta=False, emit_kv=False, tile=ROW_TILE,
                                   name=f"prompt_l{l}", gcar_in=outs_m[1][0],
                                   attn=(kd_p, vd_p, kd_m, vd_m, *attn_w), ln=ln, ffn=ffn,
                                   cast_next=cast_next)
            qexp = _expand_q(q_s.astype(BF16))
            if l == N_A:
                cols = lambda a: jnp.transpose(a.reshape(n_dec // SAMPLE_GROUP, SAMPLE_GROUP, KV_DIM), (0, 2, 1))
                res, kt, vt = _sample_attn(qexp, kt, vt, sinkb, l, f"sample_attn_l{l}",
                                           new_cols=(cols(kv_s[:, :KV_DIM]), cols(kv_s[:, KV_DIM:])))
            else:
                res = _sample_attn(qexp, kt, vt, sinkb, l, f"sample_attn_l{l}")[0]
            emit_q = l + 1 < DEPTH
            outs_s = _sample_layer(hs, layer=l, kind="attn", emit_kv=False, emit_q=emit_q, name=f"sample_l{l}",
                                   mixer=(_contract_o(res), wao_b, bao), ln=ln, ffn=ffn, conv_t=conv_t,
                                   qproj=(wq_b, bq))
            if emit_q:
                q_s = outs_s[2]
        new_conv_p.append(outs_p[1][:, SUBLANES - (CONV_W - 1):])
        new_conv_t.append(jnp.stack([conv_t[l, 1], outs_s[1]]))
        hm, hp, hs = outs_m[0], outs_p[0], outs_s[0]
        if cast_next:
            wi_b = outs_p[-2].reshape(D_MODEL, 2 * D_FF)
            wo_b = outs_p[-1].reshape(D_FF, D_MODEL)

    unwin = lambda t: jnp.transpose(t.reshape(-1, N_KV_HEADS, HEAD_DIM, WINDOW), (0, 3, 1, 2))
    new_k_p = kvtail[:, :, :KV_DIM].reshape(B, WINDOW, N_KV_HEADS, HEAD_DIM)
    new_v_p = kvtail[:, :, KV_DIM:].reshape(B, WINDOW, N_KV_HEADS, HEAD_DIM)
    return (hp, hs.reshape(n_dec, 1, D_MODEL),
            jnp.stack(new_pool_p), jnp.transpose(jnp.stack(new_pool_t), (0, 2, 1, 3)),
            jnp.stack(new_conv_p), jnp.transpose(jnp.stack(new_conv_t), (0, 2, 1, 3)),
            new_k_p, new_v_p, unwin(kt), unwin(vt))
```

```python
import functools

import jax
import jax.numpy as jnp
from jax import lax
from jax.experimental import pallas as pl
from jax.experimental.pallas import tpu as pltpu

F32 = jnp.float32
BF16 = jnp.bfloat16

D_MODEL = 1024
DEPTH = 4
N_META = 16
N_A = DEPTH // 2
POOL_WINDOWS = (2, 4, 8, 16)
N_POOL_GROUPS = len(POOL_WINDOWS)
POOL_GC = D_MODEL // N_POOL_GROUPS
MAX_POOL = max(POOL_WINDOWS)
POOL_STATE = MAX_POOL - 1
HEAD_DIM = 64
N_HEADS = D_MODEL // HEAD_DIM
N_KV_HEADS = 4
GROUP = N_HEADS // N_KV_HEADS
KV_DIM = N_KV_HEADS * HEAD_DIM
WINDOW = 128
BLOCK = 128
ATTN_SCALE = HEAD_DIM ** -0.5
D_FF = 2816
CONV_W = 3
ALPHA = (2.0 * DEPTH) ** 0.25
LN_EPS = 1e-5
NEG = -1e30

SUBLANES = 8
LANES = 128
PAD_ROWS = BLOCK - N_META
FF_CHUNK = 256
N_FF_CHUNKS = D_FF // FF_CHUNK
OUT_BLOCK = 256
N_OUT_BLOCKS = D_MODEL // OUT_BLOCK
DEC_CHUNK = D_FF // 2
N_DEC_CHUNKS = D_FF // DEC_CHUNK
CAST_SLABS = 16
DOWN_ROWS = 256
GATE_ROWS = 128
ROW_TILE = 512
SAMPLE_GROUP = 16
DUP = 2 * HEAD_DIM
KV_DUP = N_KV_HEADS * DUP
VMEM_LIMIT = 58 * 1024 * 1024


def _layer_norm(z, g, b):
    mu = jnp.mean(z, axis=-1, keepdims=True)
    zc = z - mu
    var = jnp.mean(zc * zc, axis=-1, keepdims=True)
    return zc * lax.rsqrt(var + LN_EPS) * g + b


def _silu(c):
    return c * (1.0 / (1.0 + jnp.exp(-c)))


def _dot(a, b):
    return jnp.dot(a, b, preferred_element_type=F32)


def _const_spec(a):
    nd = a.ndim
    return pl.BlockSpec(a.shape, lambda *_: (0,) * nd, pipeline_mode=pl.Buffered(1))


def _layer_spec(a, l):
    nd = a.ndim
    return pl.BlockSpec((None,) + a.shape[1:], lambda *_: (l,) + (0,) * (nd - 1),
                        pipeline_mode=pl.Buffered(1))


def _prompt_kernel(*refs, kind, is_meta, emit_kv, tile, n_tiles, n_steps, attn_idx, n_cast):
    it = iter(refs)
    x_ref = next(it)
    halo_in_ref = next(it) if kind == "pool" else None
    gcar_in_ref = next(it)
    if kind == "pool":
        pw_ref, ps_ref = next(it), next(it)
    else:
        kcur_ref, vcur_ref, kprev_ref, vprev_ref, kmeta_ref, vmeta_ref = (next(it) for _ in range(6))
        mask_ref, sinks_ref = next(it), next(it)
        wq_ref, bq_ref, wao_ref, bao_ref = (next(it) for _ in range(4))
    lmg_ref, lmb_ref, lfg_ref, lfb_ref = (next(it) for _ in range(4))
    wi_ref, cw_ref, cb_ref, wo_ref = (next(it) for _ in range(4))
    if emit_kv:
        wkvd_ref, bkvd_ref, wkv_ref, bkv_ref = (next(it) for _ in range(4))
    cast_in_refs = [next(it) for _ in range(n_cast)]
    y_ref, gtail_ref = next(it), next(it)
    if emit_kv:
        kd_ref, vd_ref, kvtail_ref = next(it), next(it), next(it)
    for src_ref in cast_in_refs:
        next(it)[...] = src_ref[...].astype(BF16)
    gcar_ref, hh_ref, h1_ref, hb_ref = (next(it) for _ in range(4))
    if kind == "pool":
        halo_ref = next(it)
    else:
        qb_ref, ob_ref = next(it), next(it)
    if emit_kv:
        h2b_ref = next(it)

    T = tile
    n_rb = T // BLOCK
    s = pl.program_id(0)
    i = s % n_tiles
    has_cur = s < n_steps - 1

    @pl.when(s == 0)
    def _():
        hh_ref[...] = jnp.zeros_like(hh_ref)
        h1_ref[...] = jnp.zeros_like(h1_ref)

    @pl.when(jnp.logical_and(has_cur, i == 0))
    def _():
        gcar_ref[...] = gcar_in_ref[...]
        if kind == "pool":
            halo_ref[...] = halo_in_ref[...]

    def block_rows(r):
        return slice(r * BLOCK, (r + 1) * BLOCK)

    def live_rows(r):
        return lax.broadcasted_iota(jnp.int32, (BLOCK, 1), 0) + r * BLOCK >= PAD_ROWS

    down_rows = min(DOWN_ROWS, T)
    down_ids = [(n, m) for m in range(T // down_rows) for n in range(N_OUT_BLOCKS)]

    def down_piece(f_pieces, n, m):
        f_pieces[n, m] = _dot(hh_ref[m * down_rows:(m + 1) * down_rows, :],
                              wo_ref[:, n * OUT_BLOCK:(n + 1) * OUT_BLOCK])

    def finish_rows(r, f_pieces, h1_old_ref):
        rows = block_rows(r)
        m, off = divmod(r * BLOCK, down_rows)
        f = jnp.concatenate([f_pieces[n, m][off:off + BLOCK] for n in range(N_OUT_BLOCKS)], axis=1)
        h2 = _layer_norm(ALPHA * h1_old_ref[rows, :] + f, lfg_ref[...], lfb_ref[...])
        y_ref[0, rows, :] = h2
        if emit_kv:
            h2b_ref[rows, :] = h2.astype(BF16)

    def kv_block(n):
        cols = slice(n * OUT_BLOCK, (n + 1) * OUT_BLOCK)
        kvd = (_dot(h2b_ref[...], wkvd_ref[:, cols]) + bkvd_ref[:, cols]).astype(BF16)
        if n * OUT_BLOCK < KV_DUP:
            kd_ref[0, :, cols] = kvd
        else:
            vd_ref[0, :, n * OUT_BLOCK - KV_DUP:(n + 1) * OUT_BLOCK - KV_DUP] = kvd

    def first_norm(r, x_blk, mix, h1_new_ref):
        rows = block_rows(r)
        h1 = _layer_norm(ALPHA * x_blk + mix, lmg_ref[...], lmb_ref[...])
        h1_new_ref[rows, :] = h1
        hb_ref[rows, :] = h1.astype(BF16)

    def pool_rows(r, h1_new_ref):
        rows = block_rows(r)
        x_blk = x_ref[0, rows, :]
        if is_meta:
            x_blk = jnp.where(live_rows(r), x_blk, 0.0)
        if r == 0:
            xe = jnp.concatenate([halo_ref[...], x_blk], axis=0)
        else:
            xe = x_ref[0, r * BLOCK - MAX_POOL:(r + 1) * BLOCK, :]
        s2 = xe + pltpu.roll(xe, 1, 0)
        s2r = s2[:, POOL_GC:]
        s4 = s2r + pltpu.roll(s2r, 2, 0)
        s4r = s4[:, POOL_GC:]
        s8 = s4r + pltpu.roll(s4r, 4, 0)
        s8r = s8[:, POOL_GC:]
        s16 = s8r + pltpu.roll(s8r, 8, 0)
        wins = (s2[MAX_POOL:, :POOL_GC], s4[MAX_POOL:, :POOL_GC],
                s8[MAX_POOL:, :POOL_GC], s16[MAX_POOL:])
        parts = []
        for gi, w in enumerate(POOL_WINDOWS):
            xg = x_blk[:, gi * POOL_GC:(gi + 1) * POOL_GC]
            if is_meta:
                t1 = lax.broadcasted_iota(jnp.int32, (BLOCK, 1), 0) + (r * BLOCK - PAD_ROWS + 1)
                dg = wins[gi] / jnp.clip(t1, 1, w).astype(F32) - xg
            else:
                dg = wins[gi] * (1.0 / w) - xg
            parts.append(_dot(dg.astype(BF16), pw_ref[gi]))
        mix = jnp.concatenate(parts, axis=1) * ps_ref[...]
        first_norm(r, x_blk, mix, h1_new_ref)

    def q_rows(r):
        rows = block_rows(r)
        q = (_dot(x_ref[0, rows, :].astype(BF16), wq_ref[...]) + bq_ref[...]) * ATTN_SCALE
        qb_ref[rows, :] = q.astype(BF16)

    lo = lax.broadcasted_iota(jnp.int32, (BLOCK, LANES), 1) < HEAD_DIM

    def keys_values(j, h):
        cols = slice(h * DUP, (h + 1) * DUP)
        if j == 0:
            first = i == 0
            kp = jnp.where(first, kmeta_ref[:, cols], kprev_ref[0, :, cols])
            vp = jnp.where(first, vmeta_ref[:, cols], vprev_ref[0, :, cols])
            return (jnp.concatenate([kp, kcur_ref[0, 0:BLOCK, cols]], axis=0),
                    jnp.concatenate([vp, vcur_ref[0, 0:BLOCK, cols]], axis=0))
        return (kcur_ref[0, (j - 1) * BLOCK:(j + 1) * BLOCK, cols],
                vcur_ref[0, (j - 1) * BLOCK:(j + 1) * BLOCK, cols])

    def scores(j, h):
        rows = block_rows(j)
        c0 = h * GROUP * HEAD_DIM
        qa = qb_ref[rows, c0:c0 + LANES]
        qc = qb_ref[rows, c0 + LANES:c0 + 2 * LANES]
        z = jnp.zeros_like(qa)
        lhs = jnp.concatenate([jnp.where(lo, qa, z), jnp.where(lo, z, qa),
                               jnp.where(lo, qc, z), jnp.where(lo, z, qc)], axis=0)
        kk, vv = keys_values(j, h)
        return lax.dot_general(lhs, kk, (((1,), (1,)), ((), ())), preferred_element_type=F32), vv

    def softmax(j, h, sc):
        if is_meta:
            madd = mask_ref[2]
        elif j == 0:
            madd = mask_ref[jnp.where(i == 0, 1, 0)]
        else:
            madd = mask_ref[0]
        es, invs = [], []
        for g in range(GROUP):
            sg = sc[g * BLOCK:(g + 1) * BLOCK] + madd
            sink = sinks_ref[attn_idx, h * GROUP + g]
            m = jnp.maximum(jnp.max(sg, axis=-1, keepdims=True), sink)
            e = jnp.exp(sg - m)
            den = jnp.sum(e, axis=-1, keepdims=True) + jnp.exp(sink - m)
            es.append(e.astype(BF16))
            invs.append(1.0 / den)
        return jnp.concatenate(es, axis=0), jnp.concatenate(invs, axis=0)

    def weighted_values(j, h, p_pair, v_pair):
        rows = block_rows(j)
        res2 = _dot(jnp.concatenate([e for e, _ in p_pair], axis=0), jnp.concatenate(v_pair, axis=1))
        for k in range(2):
            c0 = (h + k) * GROUP * HEAD_DIM
            res = res2[k * GROUP * BLOCK:(k + 1) * GROUP * BLOCK, k * DUP:(k + 1) * DUP] * p_pair[k][1]
            oa = jnp.where(lo, res[0:BLOCK], res[BLOCK:2 * BLOCK])
            oc = jnp.where(lo, res[2 * BLOCK:3 * BLOCK], res[3 * BLOCK:])
            ob_ref[rows, c0:c0 + LANES] = oa.astype(BF16)
            ob_ref[rows, c0 + LANES:c0 + 2 * LANES] = oc.astype(BF16)

    def attn_out_rows(r, h1_new_ref):
        rows = block_rows(r)
        mix = _dot(ob_ref[rows, :], wao_ref[...]) + bao_ref[...]
        first_norm(r, x_ref[0, rows, :], mix, h1_new_ref)

    gu_rows = min(GATE_ROWS, T)
    gu_ids = [(c, m) for c in range(N_FF_CHUNKS) for m in range(T // gu_rows)]

    def gate_up(c, m):
        hb = hb_ref[m * gu_rows:(m + 1) * gu_rows, :]
        return (_dot(hb, wi_ref[:, c * FF_CHUNK:(c + 1) * FF_CHUNK]),
                _dot(hb, wi_ref[:, D_FF + c * FF_CHUNK:D_FF + (c + 1) * FF_CHUNK]))

    def conv_rows(c, m, g, u):
        cols = slice(c * FF_CHUNK, (c + 1) * FF_CHUNK)
        if is_meta:
            g = jnp.where(jnp.concatenate([live_rows(r) for r in range(n_rb)], axis=0), g, 0.0)
        gext = jnp.concatenate([gcar_ref[:, cols], g], axis=0)
        g1 = pltpu.roll(gext, 1, 0)[SUBLANES:]
        g2 = pltpu.roll(gext, 2, 0)[SUBLANES:]
        gcar_ref[:, cols] = g[gu_rows - SUBLANES:]
        cc = (cb_ref[:, cols] + g2 * cw_ref[0:1, cols] + g1 * cw_ref[1:2, cols]
              + g * cw_ref[2:3, cols])
        hh_ref[m * gu_rows:(m + 1) * gu_rows, cols] = (_silu(cc) * u).astype(BF16)

    def body(h1_new_ref, h1_old_ref):
        f_pieces = {}
        downs = [functools.partial(down_piece, f_pieces, n, m) for n, m in down_ids]
        if kind == "pool":
            per = -(-len(downs) // n_rb)
            for r in range(n_rb):
                for _ in range(min(per, len(downs))):
                    downs.pop(0)()
                pool_rows(r, h1_new_ref)
            tail = x_ref[0, T - MAX_POOL:, :]
            if is_meta:
                tail = jnp.where(live_rows(n_rb - 1)[BLOCK - MAX_POOL:], tail, 0.0)
            halo_ref[...] = tail
        else:
            for r in range(n_rb):
                q_rows(r)
            units = [(j, h) for j in range(n_rb) for h in range(N_KV_HEADS)]
            per = -(-len(downs) // len(units))
            stride = max(len(units) // len(downs), 1)
            nxt = scores(*units[0])
            for p, (j, h) in enumerate(units):
                sc, vv = nxt
                if p + 1 < len(units):
                    nxt = scores(*units[p + 1])
                if p % stride == min(1, stride - 1):
                    for _ in range(min(per, len(downs))):
                        downs.pop(0)()
                if h % 2 == 0:
                    held = (softmax(j, h, sc), vv)
                else:
                    weighted_values(j, h - 1, (held[0], softmax(j, h, sc)), (held[1], vv))
                if h == N_KV_HEADS - 1:
                    attn_out_rows(j, h1_new_ref)
        for job in downs:
            job()

        under = [functools.partial(finish_rows, r, f_pieces, h1_old_ref) for r in range(n_rb)]
        if emit_kv:
            under += [functools.partial(kv_block, n) for n in range(2 * KV_DUP // OUT_BLOCK)]
        nxt = gate_up(*gu_ids[0])
        for k, (c, m) in enumerate(gu_ids):
            g, u = nxt
            if k + 1 < len(gu_ids):
                nxt = gate_up(*gu_ids[k + 1])
            conv_rows(c, m, g, u)
            if under and (k + 1) % (T // gu_rows) == 0:
                under.pop(0)()
        for job in under:
            job()
        gtail_ref[0] = gcar_ref[...]

    @pl.when(has_cur)
    def _():
        body(h1_ref.at[s % 2], h1_ref.at[1 - s % 2])

    @pl.when(s == n_steps - 1)
    def _():
        h1_old_ref = h1_ref.at[(n_steps - 2) % 2]
        f_pieces = {}
        for n, m in down_ids:
            down_piece(f_pieces, n, m)
        for r in range(n_rb):
            finish_rows(r, f_pieces, h1_old_ref)
        if emit_kv:
            for n in range(2 * KV_DUP // OUT_BLOCK):
                kv_block(n)

    if emit_kv:
        @pl.when(jnp.logical_and(s > 0, (s - 1) % n_tiles == n_tiles - 1))
        def _():
            kvtail_ref[0] = _dot(h2b_ref[T - WINDOW:, :], wkv_ref[...]) + bkv_ref[...]


def _prompt_layer(x, *, layer, kind, is_meta, emit_kv, tile, name, halo_in=None, gcar_in,
                  pool=None, attn=None, ln, ffn, kv=None, cast_next=()):
    B, L, _ = x.shape
    n_tiles = L // tile
    n_all = B * n_tiles
    n_steps = n_all + 1
    nb = tile // BLOCK
    attn_idx = layer - N_A

    def cur(s):
        t = jnp.minimum(s, n_all - 1)
        return t // n_tiles, t % n_tiles

    def prev(s):
        t = jnp.maximum(s - 1, 0)
        return t // n_tiles, t % n_tiles

    cur_rows = lambda w: pl.BlockSpec((1, tile, w), lambda s: (*cur(s), 0))
    prev_rows = lambda w: pl.BlockSpec((1, tile, w), lambda s: (*prev(s), 0))
    ins, specs = [x], [cur_rows(D_MODEL)]

    def add(a, spec):
        ins.append(a)
        specs.append(spec)

    if kind == "pool":
        add(halo_in, _const_spec(halo_in))
    add(gcar_in, _const_spec(gcar_in))
    if kind == "pool":
        for a in pool:
            add(a, _layer_spec(a, layer))
    else:
        kd, vd, kmeta, vmeta, mask, sinks, wq, bq, wao, bao = attn

        def before(s):
            b, i = cur(s)
            return b, jnp.maximum(i * nb - 1, 0), 0

        before_spec = pl.BlockSpec((1, BLOCK, KV_DUP), before)
        ins += [kd, vd, kd, vd]
        specs += [cur_rows(KV_DUP), cur_rows(KV_DUP), before_spec, before_spec]
        for a in (kmeta, vmeta, mask):
            add(a, _const_spec(a))
        add(sinks, pl.BlockSpec(memory_space=pltpu.SMEM))
        for a in (wq, bq, wao, bao):
            add(a, _layer_spec(a, attn_idx))
    wi, cw, cb, wo = ffn
    for a in ln:
        add(a, _layer_spec(a, layer))
    add(wi, _const_spec(wi))
    add(cw, _layer_spec(cw, layer))
    add(cb, _layer_spec(cb, layer))
    add(wo, _const_spec(wo))
    if emit_kv:
        for a in kv:
            add(a, _const_spec(a))
    for a in cast_next:
        slab = a.shape[1] // CAST_SLABS
        add(a.reshape(a.shape[0], CAST_SLABS, slab, a.shape[2]),
            pl.BlockSpec((None, None, slab, a.shape[2]),
                         lambda s: (layer + 1, jnp.minimum(s, CAST_SLABS - 1), 0, 0)))

    out_shape = [jax.ShapeDtypeStruct((B, L, D_MODEL), F32),
                 jax.ShapeDtypeStruct((B, SUBLANES, D_FF), F32)]
    out_specs = [prev_rows(D_MODEL), pl.BlockSpec((1, SUBLANES, D_FF), lambda s: (cur(s)[0], 0, 0))]
    if emit_kv:
        out_shape += [jax.ShapeDtypeStruct((B, L, KV_DUP), BF16),
                      jax.ShapeDtypeStruct((B, L, KV_DUP), BF16),
                      jax.ShapeDtypeStruct((B, WINDOW, 2 * KV_DIM), F32)]
        out_specs += [prev_rows(KV_DUP), prev_rows(KV_DUP),
                      pl.BlockSpec((1, WINDOW, 2 * KV_DIM), lambda s: (prev(s)[0], 0, 0))]
    for a in cast_next:
        slab = a.shape[1] // CAST_SLABS
        out_shape.append(jax.ShapeDtypeStruct((CAST_SLABS, slab, a.shape[2]), BF16))
        out_specs.append(pl.BlockSpec((None, slab, a.shape[2]),
                                      lambda s: (jnp.minimum(s, CAST_SLABS - 1), 0, 0)))
    scratch = [pltpu.VMEM((SUBLANES, D_FF), F32), pltpu.VMEM((tile, D_FF), BF16),
               pltpu.VMEM((2, tile, D_MODEL), F32), pltpu.VMEM((tile, D_MODEL), BF16)]
    if kind == "pool":
        scratch.append(pltpu.VMEM((MAX_POOL, D_MODEL), F32))
    else:
        scratch += [pltpu.VMEM((tile, D_MODEL), BF16), pltpu.VMEM((tile, D_MODEL), BF16)]
    if emit_kv:
        scratch.append(pltpu.VMEM((tile, D_MODEL), BF16))
    body = functools.partial(_prompt_kernel, kind=kind, is_meta=is_meta, emit_kv=emit_kv,
                             tile=tile, n_tiles=n_tiles, n_steps=n_steps, attn_idx=attn_idx,
                             n_cast=len(cast_next))
    return pl.pallas_call(
        body,
        grid=(n_steps,),
        in_specs=specs,
        out_specs=out_specs,
        out_shape=out_shape,
        scratch_shapes=scratch,
        compiler_params=pltpu.CompilerParams(
            dimension_semantics=("arbitrary",), vmem_limit_bytes=VMEM_LIMIT),
        name=name,
    )(*ins)


def _sample_kernel(*refs, kind, emit_kv, emit_q):
    it = iter(refs)
    x_ref = next(it)
    if kind == "pool":
        sp_ref, pw_ref, ps_ref = next(it), next(it), next(it)
    else:
        o_ref, wao_ref, bao_ref = next(it), next(it), next(it)
    lmg_ref, lmb_ref, lfg_ref, lfb_ref = (next(it) for _ in range(4))
    wg_ref, wu_ref, cw_ref, cb_ref, wo_ref, s0_ref, s1_ref = (next(it) for _ in range(7))
    if emit_kv:
        wkv_ref, bkv_ref = next(it), next(it)
    if emit_q:
        wq_ref, bq_ref = next(it), next(it)
    y_ref, g_ref = next(it), next(it)
    if emit_kv:
        kv_ref = next(it)
    if emit_q:
        q_ref = next(it)
    h1_ref, h1b_ref, acc_ref = next(it), next(it), next(it)

    c = pl.program_id(0)

    @pl.when(c == 0)
    def _():
        x = x_ref[...]
        if kind == "pool":
            parts = []
            for gi, w in enumerate(POOL_WINDOWS):
                cols = slice(gi * POOL_GC, (gi + 1) * POOL_GC)
                win = x[:, cols]
                for k in range(1, w):
                    win = win + sp_ref[POOL_STATE - k, :, cols]
                dg = win * (1.0 / w) - x[:, cols]
                parts.append(_dot(dg.astype(BF16), pw_ref[gi]))
            mix = jnp.concatenate(parts, axis=1) * ps_ref[...]
        else:
            mix = _dot(o_ref[...], wao_ref[...]) + bao_ref[...]
        h1 = _layer_norm(ALPHA * x + mix, lmg_ref[...], lmb_ref[...])
        h1_ref[...] = h1
        h1b_ref[...] = h1.astype(BF16)
        acc_ref[...] = jnp.zeros_like(acc_ref)

    h1b = h1b_ref[...]
    g = _dot(h1b, wg_ref[...])
    u = _dot(h1b, wu_ref[...])
    g_ref[...] = g
    cc = cb_ref[...] + s0_ref[...] * cw_ref[0:1, :] + s1_ref[...] * cw_ref[1:2, :] + g * cw_ref[2:3, :]
    acc_ref[...] += _dot((_silu(cc) * u).astype(BF16), wo_ref[...])

    @pl.when(c == N_DEC_CHUNKS - 1)
    def _():
        h2 = _layer_norm(ALPHA * h1_ref[...] + acc_ref[...], lfg_ref[...], lfb_ref[...])
        y_ref[...] = h2
        h2b = h2.astype(BF16)
        if emit_kv:
            kv_ref[...] = _dot(h2b, wkv_ref[...]) + bkv_ref[...]
        if emit_q:
            q_ref[...] = _dot(h2b, wq_ref[...]) + bq_ref[...]


def _sample_layer(x, *, layer, kind, emit_kv, emit_q, name, mixer, ln, ffn, conv_t, kv=None, qproj=None):
    n = x.shape[0]
    ins, specs = [x], [_const_spec(x)]

    def add(a, spec):
        ins.append(a)
        specs.append(spec)

    if kind == "pool":
        for a in mixer:
            add(a, _layer_spec(a, layer))
    else:
        o, wao, bao = mixer
        add(o, _const_spec(o))
        add(wao, _layer_spec(wao, layer - N_A))
        add(bao, _layer_spec(bao, layer - N_A))
    for a in ln:
        add(a, _layer_spec(a, layer))
    wi, cw, cb, wo = ffn
    add(wi, pl.BlockSpec((D_MODEL, DEC_CHUNK), lambda c: (0, c)))
    add(wi, pl.BlockSpec((D_MODEL, DEC_CHUNK), lambda c: (0, N_DEC_CHUNKS + c)))
    add(cw, pl.BlockSpec((None, CONV_W, DEC_CHUNK), lambda c: (layer, 0, c)))
    add(cb, pl.BlockSpec((None, 1, DEC_CHUNK), lambda c: (layer, 0, c)))
    add(wo, pl.BlockSpec((DEC_CHUNK, D_MODEL), lambda c: (c, 0)))
    add(conv_t, pl.BlockSpec((None, None, n, DEC_CHUNK), lambda c: (layer, 0, 0, c)))
    add(conv_t, pl.BlockSpec((None, None, n, DEC_CHUNK), lambda c: (layer, 1, 0, c)))
    if emit_kv:
        for a in kv:
            add(a, _const_spec(a))
    if emit_q:
        for a in qproj:
            add(a, _layer_spec(a, layer + 1 - N_A))

    full = lambda w: pl.BlockSpec((n, w), lambda c: (0, 0))
    out_shape = [jax.ShapeDtypeStruct((n, D_MODEL), F32), jax.ShapeDtypeStruct((n, D_FF), F32)]
    out_specs = [full(D_MODEL), pl.BlockSpec((n, DEC_CHUNK), lambda c: (0, c))]
    if emit_kv:
        out_shape.append(jax.ShapeDtypeStruct((n, 2 * KV_DIM), F32))
        out_specs.append(full(2 * KV_DIM))
    if emit_q:
        out_shape.append(jax.ShapeDtypeStruct((n, D_MODEL), F32))
        out_specs.append(full(D_MODEL))
    body = functools.partial(_sample_kernel, kind=kind, emit_kv=emit_kv, emit_q=emit_q)
    return pl.pallas_call(
        body,
        grid=(N_DEC_CHUNKS,),
        in_specs=specs,
        out_specs=out_specs,
        out_shape=out_shape,
        scratch_shapes=[pltpu.VMEM((n, D_MODEL), F32), pltpu.VMEM((n, D_MODEL), BF16),
                        pltpu.VMEM((n, D_MODEL), F32)],
        compiler_params=pltpu.CompilerParams(dimension_semantics=("arbitrary",),
                                             vmem_limit_bytes=VMEM_LIMIT),
        name=name,
    )(*ins)


def _sample_attn_kernel(*refs, update):
    it = iter(refs)
    q_ref, kt_ref, vt_ref = next(it), next(it), next(it)
    if update:
        kn_ref, vn_ref = next(it), next(it)
    sink_ref, o_ref = next(it), next(it)
    if update:
        nkt_ref, nvt_ref = next(it), next(it)

    kt, vt = kt_ref[...], vt_ref[...]
    if update:
        newest = lax.broadcasted_iota(jnp.int32, (KV_DIM, WINDOW), 1) == WINDOW - 1

        def advance(win, new_ref):
            return jnp.stack([jnp.where(newest, new_ref[:, b:b + 1], pltpu.roll(win[b], WINDOW - 1, 1))
                              for b in range(SAMPLE_GROUP)])

        kt, vt = advance(kt, kn_ref), advance(vt, vn_ref)
        nkt_ref[...] = kt
        nvt_ref[...] = vt
    q = q_ref[...]
    s = jnp.einsum("bhc,bcj->bhj", q, kt.astype(BF16), preferred_element_type=F32) * ATTN_SCALE
    sink = sink_ref[...][None]
    m = jnp.max(jnp.maximum(s, sink), axis=-1, keepdims=True)
    e = jnp.exp(s - m)
    den = jnp.sum(e, axis=-1, keepdims=True) + jnp.exp(sink - m)
    p = (e / den).astype(BF16)
    o_ref[...] = jnp.einsum("bhj,bcj->bhc", p, vt.astype(BF16), preferred_element_type=F32).astype(BF16)


def _sample_attn(qexp, kt, vt, sinkb, layer, name, new_cols=None):
    n = qexp.shape[0]
    G = SAMPLE_GROUP
    update = new_cols is not None
    win_spec = pl.BlockSpec((G, KV_DIM, WINDOW), lambda i: (i, 0, 0))
    q_spec = pl.BlockSpec((G, N_HEADS, KV_DIM), lambda i: (i, 0, 0))
    ins, specs = [qexp, kt, vt], [q_spec, win_spec, win_spec]
    if update:
        col_spec = pl.BlockSpec((None, KV_DIM, G), lambda i: (i, 0, 0))
        ins += list(new_cols)
        specs += [col_spec, col_spec]
    ins.append(sinkb)
    specs.append(pl.BlockSpec((None, N_HEADS, WINDOW), lambda i: (layer - N_A, 0, 0)))
    out_shape = [jax.ShapeDtypeStruct((n, N_HEADS, KV_DIM), BF16)]
    out_specs = [q_spec]
    if update:
        out_shape += [jax.ShapeDtypeStruct(kt.shape, F32)] * 2
        out_specs += [win_spec, win_spec]
    return pl.pallas_call(
        functools.partial(_sample_attn_kernel, update=update),
        grid=(n // G,),
        in_specs=specs,
        out_specs=out_specs,
        out_shape=out_shape,
        compiler_params=pltpu.CompilerParams(dimension_semantics=("arbitrary",)),
        name=name,
    )(*ins)


def _dup_heads(w):
    lead = w.shape[:-1]
    w = w.reshape(*lead, N_KV_HEADS, 1, HEAD_DIM)
    return jnp.broadcast_to(w, (*lead, N_KV_HEADS, 2, HEAD_DIM)).reshape(*lead, KV_DUP)


def _expand_q(q):
    n = q.shape[0]
    q = q.reshape(n, N_KV_HEADS, GROUP, HEAD_DIM)
    parts = [jnp.pad(q[:, h], ((0, 0), (0, 0), (h * HEAD_DIM, KV_DIM - (h + 1) * HEAD_DIM)))
             for h in range(N_KV_HEADS)]
    return jnp.stack(parts, axis=1).reshape(n, N_HEADS, KV_DIM)


def _contract_o(res):
    n = res.shape[0]
    res = res.reshape(n, N_KV_HEADS, GROUP, N_KV_HEADS, HEAD_DIM)
    return jnp.stack([res[:, h, :, h, :] for h in range(N_KV_HEADS)], axis=1).reshape(n, N_HEADS * HEAD_DIM)


def _band_masks():
    i = jnp.arange(BLOCK)[:, None]
    j = jnp.arange(2 * BLOCK)[None, :]
    band = (j > i) & (j <= i + BLOCK)
    masks = jnp.stack([band, band & (j >= PAD_ROWS), band & (j >= BLOCK + PAD_ROWS)])
    return jnp.where(masks, 0.0, NEG).astype(F32)


def kernel(x_prompt, x_sample, state_pool, state_conv, state_k_win, state_v_win, meta_tokens, pool_w, pool_scale, w_kv, b_kv, attn_w_q, attn_b_q, attn_sinks, attn_w_o, attn_b_o, ffn_w_in, ffn_conv_w, ffn_conv_b, ffn_w_out, ln_mix_g, ln_mix_b, ln_ffn_g, ln_ffn_b):
    B, S, _ = x_prompt.shape
    n_dec = x_sample.shape[0]
    assert x_sample.shape[1] == 1 and S % ROW_TILE == 0 and n_dec % SAMPLE_GROUP == 0

    row = lambda a: a[:, None, :]
    wi_b, wo_b = ffn_w_in[0].astype(BF16), ffn_w_out[0].astype(BF16)
    conv_b = row(ffn_conv_b)
    ln =(row(ln_mix_g), row(ln_mix_b), row(ln_ffn_g), row(ln_ffn_b))
    pool = (pool_w.astype(BF16), row(pool_scale))
    wkv_b, bkv = w_kv.astype(BF16), b_kv[None, :]
    wkvd_b = jnp.concatenate([_dup_heads(w_kv[:, :KV_DIM]), _dup_heads(w_kv[:, KV_DIM:])], axis=1).astype(BF16)
    bkvd = jnp.concatenate([_dup_heads(b_kv[:KV_DIM]), _dup_heads(b_kv[KV_DIM:])])[None, :]
    kvw = (wkvd_b, bkvd, wkv_b, bkv)
    wq_b, bq = attn_w_q.astype(BF16), row(attn_b_q)
    wao_b, bao = attn_w_o.astype(BF16), row(attn_b_o)
    sinkb = jnp.broadcast_to(attn_sinks[:, :, None], (attn_sinks.shape[0], N_HEADS, WINDOW))
    masks = _band_masks()

    zeros_halo = jnp.zeros((MAX_POOL, D_MODEL), F32)
    zeros_gcar = jnp.zeros((SUBLANES, D_FF), F32)
    zeros_kv = jnp.zeros((BLOCK, KV_DUP), BF16)

    hm = jnp.concatenate([jnp.zeros((PAD_ROWS, D_MODEL), F32), meta_tokens.astype(F32)])[None]
    hp = x_prompt
    hs = x_sample.reshape(n_dec, D_MODEL)
    pool_t = jnp.transpose(state_pool, (0, 2, 1, 3))
    conv_t = jnp.transpose(state_conv, (0, 2, 1, 3))
    kt = jnp.transpose(state_k_win, (0, 2, 3, 1)).reshape(n_dec, KV_DIM, WINDOW)
    vt = jnp.transpose(state_v_win, (0, 2, 3, 1)).reshape(n_dec, KV_DIM, WINDOW)

    new_pool_p, new_pool_t, new_conv_p, new_conv_t = [], [], [], []
    for l in range(DEPTH):
        ffn = (wi_b, ffn_conv_w, conv_b, wo_b)
        cast_next = (ffn_w_in, ffn_w_out) if l + 1 < DEPTH else ()
        if l < N_A:
            emit_kv = l == N_A - 1
            new_pool_p.append(hp[:, S - POOL_STATE:])
            new_pool_t.append(jnp.concatenate([pool_t[l, 1:], hs[None]], axis=0))
            outs_m = _prompt_layer(hm, layer=l, kind="pool", is_meta=True, emit_kv=emit_kv, tile=BLOCK,
                                   name=f"meta_l{l}", halo_in=zeros_halo, gcar_in=zeros_gcar,
                                   pool=pool, ln=ln, ffn=ffn, kv=kvw)
            outs_p = _prompt_layer(hp, layer=l, kind="pool", is_meta=False, emit_kv=emit_kv, tile=ROW_TILE,
                                   name=f"prompt_l{l}", halo_in=hm[0, PAD_ROWS:], gcar_in=outs_m[1][0],
                                   pool=pool, ln=ln, ffn=ffn, kv=kvw, cast_next=cast_next)
            outs_s = _sample_layer(hs, layer=l, kind="pool", emit_kv=emit_kv, emit_q=emit_kv,
                                   name=f"sample_l{l}", mixer=(pool_t, *pool), ln=ln, ffn=ffn, conv_t=conv_t,
                                   kv=(wkv_b, bkv), qproj=(wq_b, bq))
            if emit_kv:
                kd_m, vd_m = outs_m[2][0], outs_m[3][0]
                kd_p, vd_p, kvtail = outs_p[2], outs_p[3], outs_p[4]
                kv_s, q_s = outs_s[2], outs_s[3]
        else:
            attn_w = (masks, attn_sinks, wq_b, bq, wao_b, bao)
            outs_m = _prompt_layer(hm, layer=l, kind="attn", is_meta=True, emit_kv=False, tile=BLOCK,
                                   name=f"meta_l{l}", gcar_in=zeros_gcar,
                                   attn=(kd_m[None], vd_m[None], zeros_kv, zeros_kv, *attn_w), ln=ln, ffn=ffn)
            outs_p = _prompt_layer(hp, layer=l, kind="attn", is_meta=False, emit_kv=False, tile=ROW_TILE,
                                   name=f"prompt_l{l}", gcar_in=outs_m[1][0],
                                   attn=(kd_p, vd_p, kd_m, vd_m, *attn_w), ln=ln, ffn=ffn,
                                   cast_next=cast_next)
            qexp = _expand_q(q_s.astype(BF16))
            if l == N_A:
                cols = lambda a: jnp.transpose(a.reshape(n_dec // SAMPLE_GROUP, SAMPLE_GROUP, KV_DIM), (0, 2, 1))
                res, kt, vt = _sample_attn(qexp, kt, vt, sinkb, l, f"sample_attn_l{l}",
                                           new_cols=(cols(kv_s[:, :KV_DIM]), cols(kv_s[:, KV_DIM:])))
            else:
                res = _sample_attn(qexp, kt, vt, sinkb, l, f"sample_attn_l{l}")[0]
            emit_q = l + 1 < DEPTH
            outs_s = _sample_layer(hs, layer=l, kind="attn", emit_kv=False, emit_q=emit_q, name=f"sample_l{l}",
                                   mixer=(_contract_o(res), wao_b, bao), ln=ln, ffn=ffn, conv_t=conv_t,
                                   qproj=(wq_b, bq))
            if emit_q:
                q_s = outs_s[2]
        new_conv_p.append(outs_p[1][:, SUBLANES - (CONV_W - 1):])
        new_conv_t.append(jnp.stack([conv_t[l, 1], outs_s[1]]))
        hm, hp, hs = outs_m[0], outs_p[0], outs_s[0]
        if cast_next:
            wi_b = outs_p[-2].reshape(D_MODEL, 2 * D_FF)
            wo_b = outs_p[-1].reshape(D_FF, D_MODEL)

    unwin = lambda t: jnp.transpose(t.reshape(-1, N_KV_HEADS, HEAD_DIM, WINDOW), (0, 3, 1, 2))
    new_k_p = kvtail[:, :, :KV_DIM].reshape(B, WINDOW, N_KV_HEADS, HEAD_DIM)
    new_v_p = kvtail[:, :, KV_DIM:].reshape(B, WINDOW, N_KV_HEADS, HEAD_DIM)
    return (hp, hs.reshape(n_dec, 1, D_MODEL),
            jnp.stack(new_pool_p), jnp.transpose(jnp.stack(new_pool_t), (0, 2, 1, 3)),
            jnp.stack(new_conv_p), jnp.transpose(jnp.stack(new_conv_t), (0, 2, 1, 3)),
            new_k_p, new_v_p, unwin(kt), unwin(vt))
```
